```python
import math
import jax, jax.numpy as jnp
from jax import lax
import numpy as np

D_MODEL = 1024
BATCH = 32
SEQ = 2048
DEPTH = 4

N_MIXERS = 3
HEAD_DIM = 64
DIFF_HEADS = D_MODEL // (2 * HEAD_DIM)
GQA_HEADS = D_MODEL // HEAD_DIM
GQA_KV_HEADS = GQA_HEADS // 4
GRID_W = 64
WINDOW = 128
Q_BLOCK = 128
ROPE_THETA = 10000.0
N_EXPERTS = 16
EC_CAPACITY_FACTOR = 2
D_FF_EXPERT = 2 * D_MODEL
RMS_EPS = 1e-6
N_LAYERS_A = (DEPTH + 2) // 3
N_LAYERS_B = (DEPTH + 1) // 3
N_LAYERS_C = DEPTH // 3

kernel_name = "hybrid_diff_axial_window_ec_moe_encoder"


def rms_norm(x, g):
    xf = x.astype(jnp.float32)
    y = xf * lax.rsqrt(jnp.mean(xf * xf, axis=-1, keepdims=True) + RMS_EPS)
    return (y * g.astype(jnp.float32)).astype(x.dtype)


def rope_tables(pos, dim):
    inv = ROPE_THETA ** (-jnp.arange(0, dim, 2, dtype=jnp.float32) / dim)
    ang = pos.astype(jnp.float32)[:, None] * inv[None, :]
    return jnp.cos(ang), jnp.sin(ang)


def apply_rope(x, cos, sin):
    half = x.shape[-1] // 2
    x1, x2 = x[..., :half], x[..., half:]
    c = cos[None, :, None, :].astype(x.dtype)
    s = sin[None, :, None, :].astype(x.dtype)
    return jnp.concatenate([x1 * c - x2 * s, x2 * c + x1 * s], axis=-1)


def apply_axial_rope(x, row_cos, row_sin, col_cos, col_sin):
    half = x.shape[-1] // 2
    return jnp.concatenate([apply_rope(x[..., :half], row_cos, row_sin),
                            apply_rope(x[..., half:], col_cos, col_sin)], axis=-1)


def diff_attention(h, w_in, w_o, lq1, lk1, lq2, lk2, subln_g, lambda_init, cos, sin):
    B, S, _ = h.shape
    H, d = DIFF_HEADS, HEAD_DIM
    q, k, v = jnp.split(h @ w_in.astype(h.dtype), 3, axis=-1)
    q = apply_rope(q.reshape(B, S, 2 * H, d), cos, sin).reshape(B, S, H, 2, d)
    k = apply_rope(k.reshape(B, S, 2 * H, d), cos, sin).reshape(B, S, H, 2, d)
    v = v.reshape(B, S, H, 2 * d)
    lam = (jnp.exp(jnp.sum(lq1.astype(jnp.float32) * lk1.astype(jnp.float32)))
           - jnp.exp(jnp.sum(lq2.astype(jnp.float32) * lk2.astype(jnp.float32)))
           + lambda_init)
    nb = S // Q_BLOCK
    qb = q.reshape(B, nb, Q_BLOCK, H, 2, d).swapaxes(0, 1)
    scale = d ** -0.5

    def block(qblk):
        s = jnp.einsum('bqhcd,bshcd->bhcqs', qblk, k).astype(jnp.float32) * scale
        p = jax.nn.softmax(s, axis=-1)
        a = (p[:, :, 0] - lam * p[:, :, 1]).astype(v.dtype)
        return jnp.einsum('bhqs,bshe->bqhe', a, v)

    o = lax.map(block, qb).swapaxes(0, 1).reshape(B, S, H, 2 * d)
    o = rms_norm(o, subln_g) * (1.0 - lambda_init)
    return o.reshape(B, S, H * 2 * d) @ w_o.astype(h.dtype)


def split_gqa(h, w_in):
    B, S, _ = h.shape
    H, KV, d = GQA_HEADS, GQA_KV_HEADS, HEAD_DIM
    q, k, v = jnp.split(h @ w_in.astype(h.dtype), [H * d, H * d + KV * d], axis=-1)
    return q.reshape(B, S, H, d), k.reshape(B, S, KV, d), v.reshape(B, S, KV, d)


def axial_gqa(h, w_in, w_o, q_norm_g, k_norm_g, row_cos, row_sin, col_cos, col_sin):
    B, S, _ = h.shape
    H, KV, d = GQA_HEADS, GQA_KV_HEADS, HEAD_DIM
    G = H // KV
    q, k, v = split_gqa(h, w_in)
    q = apply_axial_rope(rms_norm(q, q_norm_g), row_cos, row_sin, col_cos, col_sin)
    k = apply_axial_rope(rms_norm(k, k_norm_g), row_cos, row_sin, col_cos, col_sin)
    nb = S // Q_BLOCK
    qb = q.reshape(B, nb, Q_BLOCK, KV, G, d).swapaxes(0, 1)
    scale = d ** -0.5

    def block(qblk):
        s = jnp.einsum('bqkgd,bskd->bkgqs', qblk, k).astype(jnp.float32) * scale
        p = jax.nn.softmax(s, axis=-1).astype(v.dtype)
        return jnp.einsum('bkgqs,bskd->bqkgd', p, v)

    o = lax.map(block, qb).swapaxes(0, 1).reshape(B, S, H * d)
    return o @ w_o.astype(h.dtype)


def window_gqa_sink(h, w_in, w_o, sinks, cos, sin):
    B, S, _ = h.shape
    H, KV, d = GQA_HEADS, GQA_KV_HEADS, HEAD_DIM
    G = H // KV
    L = Q_BLOCK + 2 * WINDOW
    q, k, v = split_gqa(h, w_in)
    q = apply_rope(q, cos, sin).reshape(B, S, KV, G, d)
    k = apply_rope(k, cos, sin)
    pad = ((0, 0), (WINDOW, WINDOW), (0, 0), (0, 0))
    kp, vp = jnp.pad(k, pad), jnp.pad(v, pad)
    sink = sinks.astype(jnp.float32).reshape(KV, G)
    scale = d ** -0.5

    def block(i):
        start = i * Q_BLOCK
        qblk = lax.dynamic_slice_in_dim(q, start, Q_BLOCK, axis=1)
        kblk = lax.dynamic_slice_in_dim(kp, start, L, axis=1)
        vblk = lax.dynamic_slice_in_dim(vp, start, L, axis=1)
        s = jnp.einsum('bqkgd,blkd->bkgql', qblk, kblk).astype(jnp.float32) * scale
        qpos = start + jnp.arange(Q_BLOCK)
        kpos = start - WINDOW + jnp.arange(L)
        valid = ((jnp.abs(qpos[:, None] - kpos[None, :]) <= WINDOW)
                 & (kpos[None, :] >= 0) & (kpos[None, :] < S))
        s = jnp.where(valid, s, -jnp.inf)
        sink_col = jnp.broadcast_to(sink[None, :, :, None, None], s.shape[:-1] + (1,))
        p = jax.nn.softmax(jnp.concatenate([s, sink_col], axis=-1), axis=-1)[..., :-1]
        return jnp.einsum('bkgql,blkd->bqkgd', p.astype(v.dtype), vblk)

    o = lax.map(block, jnp.arange(S // Q_BLOCK))
    o = o.swapaxes(0, 1).reshape(B, S, H * d)
    return o @ w_o.astype(h.dtype)


def expert_choice_ffn(h, w_router, w_gate, w_up, w_down):
    B, N, D = h.shape
    cap = EC_CAPACITY_FACTOR * N // N_EXPERTS
    aff = jax.nn.softmax((h @ w_router.astype(h.dtype)).astype(jnp.float32), axis=-1)
    gates, idx = lax.top_k(aff.swapaxes(1, 2), cap)

    def expert(args):
        idx_e, gate_e, wg, wu, wd = args
        xe = jax.vmap(lambda hb, ib: hb[ib])(h, idx_e)
        hid = jax.nn.silu(xe @ wg.astype(h.dtype)) * (xe @ wu.astype(h.dtype))
        return ((hid @ wd.astype(h.dtype)) * gate_e[..., None]).astype(h.dtype)

    ye = lax.map(expert, (idx.swapaxes(0, 1), gates.swapaxes(0, 1), w_gate, w_up, w_down))
    return jax.vmap(lambda ib, cb: jnp.zeros((N, D), h.dtype).at[ib.reshape(-1)].add(cb.reshape(-1, D)),
                    in_axes=(0, 1))(idx, ye)


def setup_inputs(seed: int = 0) -> dict:
    key = jax.random.key(seed)
    ks = jax.random.split(key, 24)
    D, d, E, F = D_MODEL, HEAD_DIM, N_EXPERTS, D_FF_EXPERT
    gqa_in = (GQA_HEADS + 2 * GQA_KV_HEADS) * d

    def normal(k, shape, scale):
        return jax.random.normal(k, shape, jnp.float32) * scale

    def gain(k, shape):
        return 1.0 + 0.02 * jax.random.normal(k, shape, jnp.float32)

    return {
        "x": normal(ks[0], (BATCH, SEQ, D), 1.0),
        "attn_norm_g": gain(ks[1], (DEPTH, D)),
        "ffn_norm_g": gain(ks[2], (DEPTH, D)),
        "final_norm_g": gain(ks[3], (D,)),
        "a_w_in": normal(ks[4], (N_LAYERS_A, D, 3 * D), D ** -0.5),
        "a_w_o": normal(ks[5], (N_LAYERS_A, D, D), D ** -0.5),
        "a_lambda_q1": normal(ks[6], (N_LAYERS_A, d), 0.1),
        "a_lambda_k1": normal(ks[7], (N_LAYERS_A, d), 0.1),
        "a_lambda_q2": normal(ks[8], (N_LAYERS_A, d), 0.1),
        "a_lambda_k2": normal(ks[9], (N_LAYERS_A, d), 0.1),
        "a_subln_g": gain(ks[10], (N_LAYERS_A, 2 * d)),
        "b_w_in": normal(ks[11], (N_LAYERS_B, D, gqa_in), D ** -0.5),
        "b_w_o": normal(ks[12], (N_LAYERS_B, GQA_HEADS * d, D), (GQA_HEADS * d) ** -0.5),
        "b_q_norm_g": gain(ks[13], (N_LAYERS_B, d)),
        "b_k_norm_g": gain(ks[14], (N_LAYERS_B, d)),
        "c_w_in": normal(ks[15], (N_LAYERS_C, D, gqa_in), D ** -0.5),
        "c_w_o": normal(ks[16], (N_LAYERS_C, GQA_HEADS * d, D), (GQA_HEADS * d) ** -0.5),
        "c_sinks": normal(ks[17], (N_LAYERS_C, GQA_HEADS), 0.5),
        "moe_w_router": normal(ks[18], (DEPTH, D, E), D ** -0.5),
        "moe_w_gate": normal(ks[19], (DEPTH, E, D, F), D ** -0.5),
        "moe_w_up": normal(ks[20], (DEPTH, E, D, F), D ** -0.5),
        "moe_w_down": normal(ks[21], (DEPTH, E, F, D), F ** -0.5),
    }


def reference(x, attn_norm_g, ffn_norm_g, final_norm_g,
              a_w_in, a_w_o, a_lambda_q1, a_lambda_k1, a_lambda_q2, a_lambda_k2, a_subln_g,
              b_w_in, b_w_o, b_q_norm_g, b_k_norm_g,
              c_w_in, c_w_o, c_sinks,
              moe_w_router, moe_w_gate, moe_w_up, moe_w_down):
    S = x.shape[1]
    pos = jnp.arange(S)
    cos, sin = rope_tables(pos, HEAD_DIM)
    rows = S // GRID_W
    row_idx = jnp.repeat(jnp.arange(rows), GRID_W)
    col_idx = jnp.tile(jnp.arange(GRID_W), rows)
    row_cos, row_sin = rope_tables(row_idx, HEAD_DIM // 2)
    col_cos, col_sin = rope_tables(col_idx, HEAD_DIM // 2)

    for i in range(DEPTH):
        m, j = i % N_MIXERS, i // N_MIXERS
        h = rms_norm(x, attn_norm_g[i])
        if m == 0:
            lambda_init = 0.8 - 0.6 * math.exp(-0.3 * i)
            mix = diff_attention(h, a_w_in[j], a_w_o[j], a_lambda_q1[j], a_lambda_k1[j],
                                 a_lambda_q2[j], a_lambda_k2[j], a_subln_g[j], lambda_init, cos, sin)
        elif m == 1:
            mix = axial_gqa(h, b_w_in[j], b_w_o[j], b_q_norm_g[j], b_k_norm_g[j],
                            row_cos, row_sin, col_cos, col_sin)
        else:
            mix = window_gqa_sink(h, c_w_in[j], c_w_o[j], c_sinks[j], cos, sin)
        x = x + mix
        x = x + expert_choice_ffn(rms_norm(x, ffn_norm_g[i]), moe_w_router[i],
                                  moe_w_gate[i], moe_w_up[i], moe_w_down[i])
    return rms_norm(x, final_norm_g)
```

```python
import functools
import math

import jax
import jax.numpy as jnp
from jax import lax
from jax.experimental import pallas as pl
from jax.experimental.pallas import tpu as pltpu

F32 = jnp.float32
BF16 = jnp.bfloat16
I32 = jnp.int32

D_MODEL = 1024
HEAD_DIM = 64
GQA_HEADS = 16
GQA_KV_HEADS = 4
GRID_W = 64
WINDOW = 128
ROPE_THETA = 10000.0
N_EXPERTS = 16
EC_CAPACITY_FACTOR = 2
RMS_EPS = 1e-6
N_MIXERS = 3

LANES = 128
VMEM_LIMIT = 56 * 1024 * 1024

TOKEN_BLOCK = 512
Q_TILE = 256
NT_DIMS = (((1,), (1,)), ((), ()))


def _params(*sem):
    return pltpu.CompilerParams(dimension_semantics=sem, vmem_limit_bytes=VMEM_LIMIT)


def _rope_tables(pos, dim):
    inv = ROPE_THETA ** (-jnp.arange(0, dim, 2, dtype=F32) / dim)
    ang = pos.astype(F32)[:, None] * inv[None, :]
    return jnp.cos(ang), jnp.sin(ang)


def _lane_tables_1d(seq):
    cos, sin = _rope_tables(jnp.arange(seq), HEAD_DIM)
    cos_l = jnp.tile(cos, (1, LANES // 32))
    sign = jnp.where((jnp.arange(LANES) & 32) == 0, -1.0, 1.0).astype(F32)
    sin_l = jnp.tile(sin, (1, LANES // 32)) * sign[None, :]
    return cos_l, sin_l


def _lane_tables_axial(seq):
    rows = seq // GRID_W
    row_idx = jnp.repeat(jnp.arange(rows), GRID_W)
    col_idx = jnp.tile(jnp.arange(GRID_W), rows)
    rc, rs = _rope_tables(row_idx, HEAD_DIM // 2)
    cc, cs = _rope_tables(col_idx, HEAD_DIM // 2)
    cos64 = jnp.concatenate([rc, rc, cc, cc], axis=1)
    sin64 = jnp.concatenate([-rs, rs, -cs, cs], axis=1)
    return jnp.tile(cos64, (1, 2)), jnp.tile(sin64, (1, 2))


def _rope_chunk(c, cos, sin, half):
    lane = lax.broadcasted_iota(I32, c.shape, 1)
    low = (lane & half) == 0
    partner = jnp.where(low, pltpu.roll(c, LANES - half, 1), pltpu.roll(c, half, 1))
    return c * cos + partner * sin


def _head_rms_chunk(c, gain):
    lane = lax.broadcasted_iota(I32, c.shape, 1)
    low = lane < HEAD_DIM
    cc = c * c
    s_lo = jnp.sum(jnp.where(low, cc, 0.0), axis=-1, keepdims=True)
    s_hi = jnp.sum(jnp.where(low, 0.0, cc), axis=-1, keepdims=True)
    ms = jnp.where(low, s_lo, s_hi) * (1.0 / HEAD_DIM)
    return c * lax.rsqrt(ms + RMS_EPS) * gain


def _proj_in_body(mixer, x_ref, g_ref, wq_ref, wk_ref, wv_ref, cos_ref, sin_ref, qg_ref, kg_ref,
                  q_ref, k_ref, v_ref):
    x = x_ref[...]
    ms = jnp.mean(x * x, axis=-1, keepdims=True)
    h = (x * lax.rsqrt(ms + RMS_EPS) * g_ref[...]).astype(BF16)
    cos = cos_ref[...]
    sin = sin_ref[...]
    half = 16 if mixer == 1 else 32

    def finish(acc, out_ref, scale, gain_ref):
        for j in range(acc.shape[1] // LANES):
            c = acc[:, j * LANES:(j + 1) * LANES]
            if mixer == 1:
                c = _head_rms_chunk(c, gain_ref[...])
            c = _rope_chunk(c, cos, sin, half)
            if scale != 1.0:
                c = c * scale
            out_ref[:, j * LANES:(j + 1) * LANES] = c.astype(BF16)

    finish(jnp.dot(h, wq_ref[...], preferred_element_type=F32), q_ref, HEAD_DIM ** -0.5, qg_ref)
    finish(jnp.dot(h, wk_ref[...], preferred_element_type=F32), k_ref, 1.0, kg_ref)
    v_ref[...] = jnp.dot(h, wv_ref[...], preferred_element_type=F32).astype(BF16)


def _proj_in(mixer, x, g, wq, wk, wv, cos_l, sin_l, qg, kg, seq):
    t, d = x.shape
    tm = TOKEN_BLOCK
    nq, nk, nv = wq.shape[1], wk.shape[1], wv.shape[1]
    sblocks = seq // tm
    row = lambda i: (i, 0)
    fixed = lambda i: (0, 0)
    tab = lambda i: (i % sblocks, 0)
    return pl.pallas_call(
        functools.partial(_proj_in_body, mixer),
        grid=(t // tm,),
        in_specs=[
            pl.BlockSpec((tm, d), row),
            pl.BlockSpec((1, d), fixed),
            pl.BlockSpec((d, nq), fixed),
            pl.BlockSpec((d, nk), fixed),
            pl.BlockSpec((d, nv), fixed),
            pl.BlockSpec((tm, LANES), tab),
            pl.BlockSpec((tm, LANES), tab),
            pl.BlockSpec((1, LANES), fixed),
            pl.BlockSpec((1, LANES), fixed),
        ],
        out_specs=[
            pl.BlockSpec((tm, nq), row),
            pl.BlockSpec((tm, nk), row),
            pl.BlockSpec((tm, nv), row),
        ],
        out_shape=[
            jax.ShapeDtypeStruct((t, nq), BF16),
            jax.ShapeDtypeStruct((t, nk), BF16),
            jax.ShapeDtypeStruct((t, nv), BF16),
        ],
        compiler_params=_params("parallel"),
        name=f"proj_in_m{mixer}",
    )(x, g, wq, wk, wv, cos_l, sin_l, qg, kg)


def _softmax_parts(s):
    m = jnp.max(s, axis=-1, keepdims=True)
    e = jnp.exp(s - m)
    return e, 1.0 / jnp.sum(e, axis=-1, keepdims=True)


def _diff_attn_body(lambda_init, q_ref, k_ref, v_ref, lq1_ref, lk1_ref, lq2_ref, lk2_ref, sg_ref, o_ref):
    q = q_ref[...]
    k = k_ref[...]
    lane = lax.broadcasted_iota(I32, q.shape, 1)
    zero = jnp.zeros_like(q)
    s1 = lax.dot_general(jnp.where(lane < HEAD_DIM, q, zero), k, NT_DIMS, preferred_element_type=F32)
    s2 = lax.dot_general(jnp.where(lane < HEAD_DIM, zero, q), k, NT_DIMS, preferred_element_type=F32)
    lam = (jnp.exp(jnp.sum(lq1_ref[...] * lk1_ref[...], axis=-1, keepdims=True))
           - jnp.exp(jnp.sum(lq2_ref[...] * lk2_ref[...], axis=-1, keepdims=True))
           + lambda_init)
    e1, r1 = _softmax_parts(s1)
    e2, r2 = _softmax_parts(s2)
    a = (e1 * r1 - e2 * (r2 * lam)).astype(BF16)
    o = jnp.dot(a, v_ref[...], preferred_element_type=F32)
    ms = jnp.mean(o * o, axis=-1, keepdims=True)
    o = o * lax.rsqrt(ms + RMS_EPS) * sg_ref[...] * (1.0 - lambda_init)
    o_ref[...] = o.astype(BF16)


def _diff_attn(q, k, v, lq1, lk1, lq2, lk2, subln_g, lambda_init):
    b, s, n = q.shape
    heads = n // LANES
    tq = Q_TILE
    vec = pl.BlockSpec((1, HEAD_DIM), lambda bi, h, i: (0, 0))
    return pl.pallas_call(
        functools.partial(_diff_attn_body, lambda_init),
        grid=(b, heads, s // tq),
        in_specs=[
            pl.BlockSpec((None, tq, LANES), lambda bi, h, i: (bi, i, h)),
            pl.BlockSpec((None, s, LANES), lambda bi, h, i: (bi, 0, h)),
            pl.BlockSpec((None, s, LANES), lambda bi, h, i: (bi, 0, h)),
            vec, vec, vec, vec,
            pl.BlockSpec((1, LANES), lambda bi, h, i: (0, 0)),
        ],
        out_specs=pl.BlockSpec((None, tq, LANES), lambda bi, h, i: (bi, i, h)),
        out_shape=jax.ShapeDtypeStruct((b, s, n), BF16),
        compiler_params=_params("parallel", "parallel", "parallel"),
        name="diff_attn",
    )(q, k, v, lq1, lk1, lq2, lk2, subln_g)


def _place_head(x, src_half, dst_half):
    return x if src_half == dst_half else pltpu.roll(x, HEAD_DIM, 1)


def _gqa_pair_body(window, q_ref, k_ref, v_ref, sink_ref, o_ref):
    tq = q_ref.shape[0]
    group = GQA_HEADS // GQA_KV_HEADS
    pair = pl.program_id(1)
    lane = lax.broadcasted_iota(I32, (tq, LANES), 1)
    if window:
        i = pl.program_id(2)
        span = tq + 2 * WINDOW
        start = pl.multiple_of(jnp.clip(i * tq - WINDOW, 0, k_ref.shape[0] - span), LANES)
        k = k_ref[pl.ds(start, span), :]
        v = v_ref[pl.ds(start, span), :]
        qpos = i * tq + lax.broadcasted_iota(I32, (tq, span), 0)
        kpos = start + lax.broadcasted_iota(I32, (tq, span), 1)
        valid = jnp.abs(qpos - kpos) <= WINDOW
    else:
        k = k_ref[...]
        v = v_ref[...]
    for c in range(2 * group // 2):
        halves = []
        for hh in range(2):
            hl = 2 * c + hh
            kvl = hl // group
            qc = q_ref[:, c * LANES:(c + 1) * LANES]
            qh = jnp.where((lane >= hh * HEAD_DIM) & (lane < (hh + 1) * HEAD_DIM), qc, jnp.zeros_like(qc))
            qh = _place_head(qh.astype(F32), hh, kvl).astype(BF16) if hh != kvl else qh
            s = lax.dot_general(qh, k, NT_DIMS, preferred_element_type=F32)
            if window:
                sink = sink_ref[pair * 2 * group + hl]
                s = jnp.where(valid, s, -jnp.inf)
                m = jnp.maximum(jnp.max(s, axis=-1, keepdims=True), sink)
                e = jnp.exp(s - m)
                r = 1.0 / (jnp.sum(e, axis=-1, keepdims=True) + jnp.exp(sink - m))
            else:
                e, r = _softmax_parts(s)
            o = jnp.dot((e * r).astype(BF16), v, preferred_element_type=F32)
            halves.append(_place_head(o, kvl, hh))
        o_ref[:, c * LANES:(c + 1) * LANES] = jnp.where(lane < HEAD_DIM, halves[0], halves[1]).astype(BF16)


def _gqa_attn(q, k, v, sinks, window):
    b, s, n = q.shape
    pairs = GQA_KV_HEADS // 2
    qw = n // pairs
    tq = Q_TILE
    return pl.pallas_call(
        functools.partial(_gqa_pair_body, window),
        grid=(b, pairs, s // tq),
        in_specs=[
            pl.BlockSpec((None, tq, qw), lambda bi, p, i: (bi, i, p)),
            pl.BlockSpec((None, s, LANES), lambda bi, p, i: (bi, 0, p)),
            pl.BlockSpec((None, s, LANES), lambda bi, p, i: (bi, 0, p)),
            pl.BlockSpec(memory_space=pltpu.SMEM),
        ],
        out_specs=pl.BlockSpec((None, tq, qw), lambda bi, p, i: (bi, i, p)),
        out_shape=jax.ShapeDtypeStruct((b, s, n), BF16),
        compiler_params=_params("parallel", "parallel", "parallel"),
        name="window_attn" if window else "axial_attn",
    )(q, k, v, sinks)


def _proj_out_body(o_ref, wo_ref, x_ref, g_ref, wr_ref, xn_ref, hp_ref, aff_ref):
    xn = x_ref[...] + jnp.dot(o_ref[...], wo_ref[...], preferred_element_type=F32)
    xn_ref[...] = xn
    ms = jnp.mean(xn * xn, axis=-1, keepdims=True)
    h = xn * lax.rsqrt(ms + RMS_EPS) * g_ref[...]
    half = h.shape[1] // 2
    hi = lax.bitcast_convert_type(h[:, :half].astype(BF16).astype(F32), jnp.uint32)
    lo = lax.bitcast_convert_type(h[:, half:].astype(BF16).astype(F32), jnp.uint32)
    hp_ref[...] = hi | (lo >> 16)
    logits = lax.dot_general(wr_ref[...], h, NT_DIMS, preferred_element_type=F32,
                             precision=lax.Precision.HIGHEST)
    m = jnp.max(logits, axis=0, keepdims=True)
    e = jnp.exp(logits - m)
    aff_ref[...] = e / jnp.sum(e, axis=0, keepdims=True)


def _proj_out(o, wo, x, g, wr_t, seq):
    t, d = x.shape
    tm = TOKEN_BLOCK
    e = wr_t.shape[0]
    sblocks = seq // tm
    row = lambda i: (i, 0)
    fixed = lambda i: (0, 0)
    return pl.pallas_call(
        _proj_out_body,
        grid=(t // tm,),
        in_specs=[
            pl.BlockSpec((tm, o.shape[1]), row),
            pl.BlockSpec(wo.shape, fixed),
            pl.BlockSpec((tm, d), row),
            pl.BlockSpec((1, d), fixed),
            pl.BlockSpec((e, d), fixed),
        ],
        out_specs=[
            pl.BlockSpec((tm, d), row),
            pl.BlockSpec((tm, d // 2), row),
            pl.BlockSpec((None, e, tm), lambda i: (i // sblocks, 0, i % sblocks)),
        ],
        out_shape=[
            jax.ShapeDtypeStruct((t, d), F32),
            jax.ShapeDtypeStruct((t, d // 2), jnp.uint32),
            jax.ShapeDtypeStruct((t // seq, e, seq), F32),
        ],
        compiler_params=_params("parallel"),
        name="proj_out_router",
    )(o, wo, x, g, wr_t)


PREFIX_CHUNK = 256


def _exclusive_prefix(mask_f):
    e, s = mask_f.shape
    r = lax.broadcasted_iota(I32, (PREFIX_CHUNK, PREFIX_CHUNK), 0)
    c = lax.broadcasted_iota(I32, (PREFIX_CHUNK, PREFIX_CHUNK), 1)
    upper = jnp.where(r < c, 1.0, 0.0).astype(BF16)
    offset = jnp.zeros((e, 1), F32)
    outs = []
    for j in range(s // PREFIX_CHUNK):
        mj = mask_f[:, j * PREFIX_CHUNK:(j + 1) * PREFIX_CHUNK]
        outs.append(jnp.dot(mj.astype(BF16), upper, preferred_element_type=F32) + offset)
        offset = offset + jnp.sum(mj, axis=-1, keepdims=True)
    return jnp.concatenate(outs, axis=1)


def _route_body(cap, aff_ref, out_ref, slot_ref):
    aff = aff_ref[...]
    e, s = aff.shape
    bits = lax.bitcast_convert_type(aff, I32)

    def count(mask):
        return jnp.sum(jnp.where(mask, 1.0, 0.0), axis=-1, keepdims=True)

    def search(i, thr):
        cand = thr | lax.shift_left(jnp.int32(1), 30 - i)
        return jnp.where(count(bits >= cand) >= cap, cand, thr)

    thr = lax.fori_loop(0, 31, search, jnp.zeros((e, 1), I32))
    gt = bits > thr
    eq = bits == thr
    need = cap - count(gt)
    eq_rank = _exclusive_prefix(jnp.where(eq, 1.0, 0.0))
    sel = gt | (eq & (eq_rank < need))
    sel_f = jnp.where(sel, 1.0, 0.0)
    slot_ref[...] = jnp.where(sel, _exclusive_prefix(sel_f), -1.0)

    slot_id = lax.broadcasted_iota(I32, (cap, s), 0).astype(F32)
    token = lax.broadcasted_iota(I32, (1, s), 1).astype(F32)
    col = lax.broadcasted_iota(I32, (cap, out_ref.shape[2]), 1)
    expert = lax.broadcasted_iota(I32, (e, s), 0)

    def per_expert(ei, carry):
        mine = expert == ei
        slot_row = jnp.sum(jnp.where(mine, slot_ref[...], 0.0), axis=0, keepdims=True)
        aff_row = jnp.sum(jnp.where(mine, aff_ref[...], 0.0), axis=0, keepdims=True)
        hit = slot_row == slot_id
        idx = jnp.sum(jnp.where(hit, token, 0.0), axis=-1, keepdims=True)
        gate = jnp.sum(jnp.where(hit, aff_row, 0.0), axis=-1, keepdims=True)
        out_ref[ei] = jnp.where(col == 0, idx, gate)
        return carry

    lax.fori_loop(0, e, per_expert, 0)


ROUTE_COLS = 8


def _route(aff, cap):
    b, e, s = aff.shape
    return pl.pallas_call(
        functools.partial(_route_body, cap),
        grid=(b,),
        in_specs=[pl.BlockSpec((None, e, s), lambda bi: (bi, 0, 0))],
        out_specs=pl.BlockSpec((None, e, cap, ROUTE_COLS), lambda bi: (bi, 0, 0, 0)),
        out_shape=jax.ShapeDtypeStruct((b, e, cap, ROUTE_COLS), F32),
        scratch_shapes=[pltpu.VMEM((e, s), F32)],
        compiler_params=_params("parallel"),
        name="ec_route",
    )(aff)


ROW_UNROLL = 8


def _gather_body(idx_ref, h_ref, o_ref):
    n = o_ref.shape[0]

    def rows(j, carry):
        base = j * ROW_UNROLL
        for u in range(ROW_UNROLL):
            o_ref[base + u] = h_ref[idx_ref[0, base + u]]
        return carry

    lax.fori_loop(0, n // ROW_UNROLL, rows, 0)


def _gather_rows(idx, hp):
    b, s, _, w = hp.shape
    n = idx.shape[2]
    return pl.pallas_call(
        _gather_body,
        grid=(b,),
        in_specs=[
            pl.BlockSpec((None, 1, n), lambda bi: (bi, 0, 0), memory_space=pltpu.SMEM),
            pl.BlockSpec((None, s, 1, w), lambda bi: (bi, 0, 0, 0)),
        ],
        out_specs=pl.BlockSpec((None, n, 1, w), lambda bi: (bi, 0, 0, 0)),
        out_shape=jax.ShapeDtypeStruct((b, n, 1, w), hp.dtype),
        compiler_params=_params("parallel"),
        name="ec_gather",
    )(idx, hp)


def _expert_body(xp_ref, gate_ref, wg_ref, wu_ref, wd_ref, y_ref):
    xp = xp_ref[...]
    x_hi = lax.bitcast_convert_type(xp & jnp.uint32(0xFFFF0000), F32).astype(BF16)
    x_lo = lax.bitcast_convert_type(xp << 16, F32).astype(BF16)
    xe = jnp.concatenate([x_hi, x_lo], axis=1)
    hg = jnp.dot(xe, wg_ref[...], preferred_element_type=F32)
    hu = jnp.dot(xe, wu_ref[...], preferred_element_type=F32)
    hid = (hg * jax.nn.sigmoid(hg) * hu).astype(BF16)
    y = jnp.dot(hid, wd_ref[...], preferred_element_type=F32)
    y_ref[...] = y * gate_ref[:, 1:2]


def _experts(xg, route, wg, wu, wd):
    b, n, half = xg.shape
    e, d, f = wg.shape
    cap = n // e
    return pl.pallas_call(
        _expert_body,
        grid=(e, b),
        in_specs=[
            pl.BlockSpec((None, cap, half), lambda ei, bi: (bi, ei, 0)),
            pl.BlockSpec((None, None, cap, ROUTE_COLS), lambda ei, bi: (bi, ei, 0, 0)),
            pl.BlockSpec((None, d, f), lambda ei, bi: (ei, 0, 0)),
            pl.BlockSpec((None, d, f), lambda ei, bi: (ei, 0, 0)),
            pl.BlockSpec((None, f, d), lambda ei, bi: (ei, 0, 0)),
        ],
        out_specs=pl.BlockSpec((None, cap, d), lambda ei, bi: (bi, ei, 0)),
        out_shape=jax.ShapeDtypeStruct((b, n, d), F32),
        compiler_params=_params("parallel", "parallel"),
        name="ec_experts",
    )(xg, route, wg, wu, wd)


def _combine_body(idx_ref, x_ref, y_ref, o_ref):
    @pl.when(pl.program_id(1) == 0)
    def _():
        o_ref[...] = x_ref[...]

    def rows(j, carry):
        base = j * ROW_UNROLL
        sums = [o_ref[idx_ref[0, base + u]] + y_ref[base + u] for u in range(ROW_UNROLL)]
        for u in range(ROW_UNROLL):
            o_ref[idx_ref[0, base + u]] = sums[u]
        return carry

    lax.fori_loop(0, y_ref.shape[0] // ROW_UNROLL, rows, 0)


def _combine(idx, x, y, cap):
    b, s, _, d = x.shape
    e = y.shape[1] // cap
    assert cap % ROW_UNROLL == 0
    return pl.pallas_call(
        _combine_body,
        grid=(b, e),
        in_specs=[
            pl.BlockSpec((None, 1, cap), lambda bi, ei: (bi, 0, ei), memory_space=pltpu.SMEM),
            pl.BlockSpec((None, s, 1, d), lambda bi, ei: (bi, 0, 0, 0)),
            pl.BlockSpec((None, cap, 1, d), lambda bi, ei: (bi, ei, 0, 0)),
        ],
        out_specs=pl.BlockSpec((None, s, 1, d), lambda bi, ei: (bi, 0, 0, 0)),
        out_shape=jax.ShapeDtypeStruct(x.shape, F32),
        compiler_params=_params("parallel", "arbitrary"),
        name="ec_combine",
    )(idx, x, y)


def _final_norm_body(x_ref, g_ref, o_ref):
    x = x_ref[...]
    ms = jnp.mean(x * x, axis=-1, keepdims=True)
    o_ref[...] = x * lax.rsqrt(ms + RMS_EPS) * g_ref[...]


def _final_norm(x, g):
    t, d = x.shape
    tm = TOKEN_BLOCK
    return pl.pallas_call(
        _final_norm_body,
        grid=(t // tm,),
        in_specs=[pl.BlockSpec((tm, d), lambda i: (i, 0)), pl.BlockSpec((1, d), lambda i: (0, 0))],
        out_specs=pl.BlockSpec((tm, d), lambda i: (i, 0)),
        out_shape=jax.ShapeDtypeStruct((t, d), F32),
        compiler_params=_params("parallel"),
        name="final_norm",
    )(x, g)


def kernel(x, attn_norm_g, ffn_norm_g, final_norm_g, a_w_in, a_w_o, a_lambda_q1, a_lambda_k1, a_lambda_q2,
           a_lambda_k2, a_subln_g, b_w_in, b_w_o, b_q_norm_g, b_k_norm_g, c_w_in, c_w_o, c_sinks,
           moe_w_router, moe_w_gate, moe_w_up, moe_w_down):
    batch, seq, d = x.shape
    depth = attn_norm_g.shape[0]
    t = batch * seq
    cap = EC_CAPACITY_FACTOR * seq // N_EXPERTS
    gqa_q = GQA_HEADS * HEAD_DIM
    gqa_kv = GQA_KV_HEADS * HEAD_DIM

    tables_1d = _lane_tables_1d(seq)
    tables_ax = _lane_tables_axial(seq)
    ones = jnp.ones((1, LANES), F32)

    xt = x.reshape(t, d)
    for i in range(depth):
        m, j = i % N_MIXERS, i // N_MIXERS
        g_attn = attn_norm_g[i].reshape(1, d)
        if m == 0:
            w = a_w_in[j].astype(BF16)
            q, k, v = _proj_in(0, xt, g_attn, w[:, :d], w[:, d:2 * d], w[:, 2 * d:], *tables_1d, ones, ones, seq)
            lambda_init = 0.8 - 0.6 * math.exp(-0.3 * i)
            vec = lambda a: a[j].reshape(1, HEAD_DIM)
            o = _diff_attn(q.reshape(batch, seq, d), k.reshape(batch, seq, d), v.reshape(batch, seq, d),
                           vec(a_lambda_q1), vec(a_lambda_k1), vec(a_lambda_q2), vec(a_lambda_k2),
                           a_subln_g[j].reshape(1, 2 * HEAD_DIM), lambda_init)
            w_o = a_w_o[j]
        else:
            w_in = b_w_in[j] if m == 1 else c_w_in[j]
            w = w_in.astype(BF16)
            if m == 1:
                tables = tables_ax
                qg = jnp.tile(b_q_norm_g[j], 2).reshape(1, LANES)
                kg = jnp.tile(b_k_norm_g[j], 2).reshape(1, LANES)
                sinks = jnp.zeros((GQA_HEADS,), F32)
            else:
                tables, qg, kg, sinks = tables_1d, ones, ones, c_sinks[j]
            q, k, v = _proj_in(m, xt, g_attn, w[:, :gqa_q], w[:, gqa_q:gqa_q + gqa_kv], w[:, gqa_q + gqa_kv:],
                               *tables, qg, kg, seq)
            o = _gqa_attn(q.reshape(batch, seq, gqa_q), k.reshape(batch, seq, gqa_kv),
                          v.reshape(batch, seq, gqa_kv), sinks, window=(m == 2))
            w_o = b_w_o[j] if m == 1 else c_w_o[j]

        xn, hp, aff = _proj_out(o.reshape(t, -1), w_o.astype(BF16), xt, ffn_norm_g[i].reshape(1, d),
                                moe_w_router[i].T, seq)
        route = _route(aff, cap)
        idx = route[..., 0].astype(I32).reshape(batch, 1, N_EXPERTS * cap)
        xg = _gather_rows(idx, hp.reshape(batch, seq, 1, d // 2))
        y = _experts(xg.reshape(batch, N_EXPERTS * cap, d // 2), route, moe_w_gate[i].astype(BF16),
                     moe_w_up[i].astype(BF16), moe_w_down[i].astype(BF16))
        xo = _combine(idx, xn.reshape(batch, seq, 1, d), y.reshape(batch, N_EXPERTS * cap, 1, d), cap)
        xt = xo.reshape(t, d)
    return _final_norm(xt, final_norm_g.reshape(1, d)).reshape(batch, seq, d)
```

```python
import functools
import math

import jax
import jax.numpy as jnp
from jax import lax
from jax.experimental import pallas as pl
from jax.experimental.pallas import tpu as pltpu

F32 = jnp.float32
BF16 = jnp.bfloat16
I32 = jnp.int32

D_MODEL = 1024
HEAD_DIM = 64
GQA_HEADS = 16
GQA_KV_HEADS = 4
GRID_W = 64
WINDOW = 128
ROPE_THETA = 10000.0
N_EXPERTS = 16
EC_CAPACITY_FACTOR = 2
RMS_EPS = 1e-6
N_MIXERS = 3

LANES = 128
VMEM_LIMIT = 56 * 1024 * 1024

TOKEN_BLOCK = 512
Q_TILE = 256
NT_DIMS = (((1,), (1,)), ((), ()))


def _params(*sem):
    return pltpu.CompilerParams(dimension_semantics=sem, vmem_limit_bytes=VMEM_LIMIT)


def _rope_tables(pos, dim):
    inv = ROPE_THETA ** (-jnp.arange(0, dim, 2, dtype=F32) / dim)
    ang = pos.astype(F32)[:, None] * inv[None, :]
    return jnp.cos(ang), jnp.sin(ang)


def _lane_tables_1d(seq):
    cos, sin = _rope_tables(jnp.arange(seq), HEAD_DIM)
    cos_l = jnp.tile(cos, (1, LANES // 32))
    sign = jnp.where((jnp.arange(LANES) & 32) == 0, -1.0, 1.0).astype(F32)
    sin_l = jnp.tile(sin, (1, LANES // 32)) * sign[None, :]
    return cos_l, sin_l


def _lane_tables_axial(seq):
    rows = seq // GRID_W
    row_idx = jnp.repeat(jnp.arange(rows), GRID_W)
    col_idx = jnp.tile(jnp.arange(GRID_W), rows)
    rc, rs = _rope_tables(row_idx, HEAD_DIM // 2)
    cc, cs = _rope_tables(col_idx, HEAD_DIM // 2)
    cos64 = jnp.concatenate([rc, rc, cc, cc], axis=1)
    sin64 = jnp.concatenate([-rs, rs, -cs, cs], axis=1)
    return jnp.tile(cos64, (1, 2)), jnp.tile(sin64, (1, 2))


def _rope_chunk(c, cos, sin, half):
    lane = lax.broadcasted_iota(I32, c.shape, 1)
    low = (lane & half) == 0
    partner = jnp.where(low, pltpu.roll(c, LANES - half, 1), pltpu.roll(c, half, 1))
    return c * cos + partner * sin


def _head_rms_chunk(c, gain):
    lane = lax.broadcasted_iota(I32, c.shape, 1)
    low = lane < HEAD_DIM
    cc = c * c
    s_lo = jnp.sum(jnp.where(low, cc, 0.0), axis=-1, keepdims=True)
    s_hi = jnp.sum(jnp.where(low, 0.0, cc), axis=-1, keepdims=True)
    ms = jnp.where(low, s_lo, s_hi) * (1.0 / HEAD_DIM)
    return c * lax.rsqrt(ms + RMS_EPS) * gain


def _proj_in_body(mixer, x_ref, g_ref, wq_ref, wk_ref, wv_ref, cos_ref, sin_ref, qg_ref, kg_ref,
                  q_ref, k_ref, v_ref, x2_ref):
    x2_ref[...] = x_ref[...].reshape(x2_ref.shape)
    x = x2_ref[...]
    ms = jnp.mean(x * x, axis=-1, keepdims=True)
    h = (x * lax.rsqrt(ms + RMS_EPS) * g_ref[...]).astype(BF16)
    cos = cos_ref[...]
    sin = sin_ref[...]
    half = 16 if mixer == 1 else 32

    def finish(acc, out_ref, scale, gain_ref):
        for j in range(acc.shape[1] // LANES):
            c = acc[:, j * LANES:(j + 1) * LANES]
            if mixer == 1:
                c = _head_rms_chunk(c, gain_ref[...])
            c = _rope_chunk(c, cos, sin, half)
            if scale != 1.0:
                c = c * scale
            out_ref[:, j * LANES:(j + 1) * LANES] = c.astype(BF16)

    finish(jnp.dot(h, wq_ref[...], preferred_element_type=F32), q_ref, Q_SCALE, qg_ref)
    finish(jnp.dot(h, wk_ref[...], preferred_element_type=F32), k_ref, 1.0, kg_ref)
    v_ref[...] = jnp.dot(h, wv_ref[...], preferred_element_type=F32).astype(BF16)


def _proj_in(mixer, x, g, wq, wk, wv, cos_l, sin_l, qg, kg, seq):
    t, _, d = x.shape
    tm = TOKEN_BLOCK
    nq, nk, nv = wq.shape[1], wk.shape[1], wv.shape[1]
    sblocks = seq // tm
    row = lambda i: (i, 0)
    fixed = lambda i: (0, 0)
    tab = lambda i: (i % sblocks, 0)
    return pl.pallas_call(
        functools.partial(_proj_in_body, mixer),
        grid=(t // tm,),
        in_specs=[
            pl.BlockSpec((tm, 1, d), lambda i: (i, 0, 0)),
            pl.BlockSpec((1, d), fixed),
            pl.BlockSpec((d, nq), fixed),
            pl.BlockSpec((d, nk), fixed),
            pl.BlockSpec((d, nv), fixed),
            pl.BlockSpec((tm, LANES), tab),
            pl.BlockSpec((tm, LANES), tab),
            pl.BlockSpec((1, LANES), fixed),
            pl.BlockSpec((1, LANES), fixed),
        ],
        out_specs=[
            pl.BlockSpec((tm, nq), row),
            pl.BlockSpec((tm, nk), row),
            pl.BlockSpec((tm, nv), row),
        ],
        out_shape=[
            jax.ShapeDtypeStruct((t, nq), BF16),
            jax.ShapeDtypeStruct((t, nk), BF16),
            jax.ShapeDtypeStruct((t, nv), BF16),
        ],
        scratch_shapes=[pltpu.VMEM((tm, d), F32)],
        compiler_params=_params("parallel"),
        name=f"proj_in_m{mixer}",
    )(x, g, wq, wk, wv, cos_l, sin_l, qg, kg)


LOG2E = 1.4426950408889634
Q_SCALE = HEAD_DIM ** -0.5 * LOG2E
KEY_CHUNK = 512


def _attention_chains(n_chains, q_of, k_of, v_aug_of, finish, s_ref, p_ref, span, mask_of=None, floor_of=None):
    kc = min(KEY_CHUNK, span)
    pieces = span // kc

    def scores(c, j, q):
        s = lax.dot_general(q, k_of(c, j), NT_DIMS, preferred_element_type=F32)
        if mask_of is not None:
            s = jnp.where(mask_of(j), s, -jnp.inf)
        s_ref[c % 2, :, j * kc:(j + 1) * kc] = s

    q = q_of(0)
    for j in range(pieces):
        scores(0, j, q)
    for c in range(n_chains):
        slot = c % 2
        m = jnp.max(s_ref[slot], axis=-1, keepdims=True)
        if floor_of is not None:
            m = jnp.maximum(m, floor_of(c))
        q = q_of(c + 1) if c + 1 < n_chains else None
        for j in range(pieces):
            if q is not None:
                scores(c + 1, j, q)
            e = jnp.exp2(s_ref[slot, :, j * kc:(j + 1) * kc] - m)
            p_ref[slot, :, j * kc:(j + 1) * kc] = e.astype(BF16)
        finish(c, jnp.dot(p_ref[slot], v_aug_of(c), preferred_element_type=F32), m)


def _fill_v_aug(vaug_ref, v_ref):
    vaug_ref[:, :LANES] = v_ref[...]
    vaug_ref[:, LANES:] = jnp.ones((vaug_ref.shape[0], LANES), BF16)


def _diff_attn_body(lambda_init, q_ref, k_ref, v_ref, lq1_ref, lk1_ref, lq2_ref, lk2_ref, sg_ref, o_ref,
                    vaug_ref, s_ref, p_ref):
    pl.when(pl.program_id(2) == 0)(functools.partial(_fill_v_aug, vaug_ref, v_ref))
    tq = s_ref.shape[1]
    span = k_ref.shape[0]
    kc = min(KEY_CHUNK, span)
    lane = lax.broadcasted_iota(I32, (tq, LANES), 1)
    lam = (jnp.exp(jnp.sum(lq1_ref[...] * lk1_ref[...], axis=-1, keepdims=True))
           - jnp.exp(jnp.sum(lq2_ref[...] * lk2_ref[...], axis=-1, keepdims=True))
           + lambda_init)

    def q_of(c):
        q = q_ref[(c // 2) * tq:(c // 2 + 1) * tq, :]
        keep = lane < HEAD_DIM if c % 2 == 0 else lane >= HEAD_DIM
        return jnp.where(keep, q, jnp.zeros_like(q))

    first = []

    def finish(c, o_aug, m):
        o = o_aug[:, :LANES] * (1.0 / o_aug[:, LANES:LANES + 1])
        if c % 2 == 0:
            first.append(o)
            return
        o = first.pop() - lam * o
        ms = jnp.mean(o * o, axis=-1, keepdims=True)
        o = o * lax.rsqrt(ms + RMS_EPS) * sg_ref[...] * (1.0 - lambda_init)
        o_ref[(c // 2) * tq:(c // 2 + 1) * tq, :] = o.astype(BF16)

    _attention_chains(2 * (q_ref.shape[0] // tq), q_of, lambda c, j: k_ref[j * kc:(j + 1) * kc, :],
                      lambda c: vaug_ref[...], finish, s_ref, p_ref, span)


DIFF_Q_TILES = 2


def _attn_scratch(tq, span, keys):
    return [pltpu.VMEM((keys, 2 * LANES), BF16), pltpu.VMEM((2, tq, span), F32), pltpu.VMEM((2, tq, span), BF16)]


def _diff_attn(q, k, v, lq1, lk1, lq2, lk2, subln_g, lambda_init):
    b, s, n = q.shape
    heads = n // LANES
    tq = Q_TILE
    rows = DIFF_Q_TILES * tq
    vec = pl.BlockSpec((1, HEAD_DIM), lambda bi, h, i: (0, 0))
    return pl.pallas_call(
        functools.partial(_diff_attn_body, lambda_init),
        grid=(b, heads, s // rows),
        in_specs=[
            pl.BlockSpec((None, rows, LANES), lambda bi, h, i: (bi, i, h)),
            pl.BlockSpec((None, s, LANES), lambda bi, h, i: (bi, 0, h)),
            pl.BlockSpec((None, s, LANES), lambda bi, h, i: (bi, 0, h)),
            vec, vec, vec, vec,
            pl.BlockSpec((1, LANES), lambda bi, h, i: (0, 0)),
        ],
        out_specs=pl.BlockSpec((None, rows, LANES), lambda bi, h, i: (bi, i, h)),
        out_shape=jax.ShapeDtypeStruct((b, s, n), BF16),
        scratch_shapes=_attn_scratch(tq, s, s),
        compiler_params=_params("parallel", "parallel", "arbitrary"),
        name="diff_attn",
    )(q, k, v, lq1, lk1, lq2, lk2, subln_g)


def _place_head(x, src_half, dst_half):
    return x if src_half == dst_half else pltpu.roll(x, HEAD_DIM, 1)


def _gqa_pair_body(window, q_ref, k_ref, v_ref, sink_ref, o_ref, vaug_ref, s_ref, p_ref):
    pl.when(pl.program_id(2) == 0)(functools.partial(_fill_v_aug, vaug_ref, v_ref))
    tq = q_ref.shape[0]
    span = s_ref.shape[2]
    kc = min(KEY_CHUNK, span)
    group = GQA_HEADS // GQA_KV_HEADS
    pair = pl.program_id(1)
    lane = lax.broadcasted_iota(I32, (tq, LANES), 1)
    if window:
        i = pl.program_id(2)
        start = pl.multiple_of(jnp.clip(i * tq - WINDOW, 0, k_ref.shape[0] - span), LANES)
        qpos = i * tq + lax.broadcasted_iota(I32, (tq, kc), 0)
        kpos = start + lax.broadcasted_iota(I32, (tq, kc), 1)
        mask_of = lambda j: jnp.abs(qpos - (kpos + j * kc)) <= WINDOW
        k_of = lambda c, j: k_ref[pl.ds(start + j * kc, kc), :]
        v_aug_of = lambda c: vaug_ref[pl.ds(start, span), :]
        sink2 = lambda c: sink_ref[pair * 2 * group + c] * LOG2E
    else:
        mask_of = sink2 = None
        k_of = lambda c, j: k_ref[j * kc:(j + 1) * kc, :]
        v_aug_of = lambda c: vaug_ref[...]

    def q_of(c):
        hh, kvl = c % 2, c // group
        qc = q_ref[:, (c // 2) * LANES:(c // 2 + 1) * LANES]
        qh = jnp.where((lane >= hh * HEAD_DIM) & (lane < (hh + 1) * HEAD_DIM), qc, jnp.zeros_like(qc))
        return qh if hh == kvl else _place_head(qh.astype(F32), hh, kvl).astype(BF16)

    first = []

    def finish(c, o_aug, m):
        hh, kvl = c % 2, c // group
        total = o_aug[:, LANES:LANES + 1]
        if window:
            total = total + jnp.exp2(sink2(c) - m)
        o = _place_head(o_aug[:, :LANES] * (1.0 / total), kvl, hh)
        if hh == 0:
            first.append(o)
        else:
            o_ref[:, (c // 2) * LANES:(c // 2 + 1) * LANES] = jnp.where(lane < HEAD_DIM, first.pop(), o).astype(BF16)

    _attention_chains(2 * group, q_of, k_of, v_aug_of, finish, s_ref, p_ref, span, mask_of, sink2)


def _gqa_attn(q, k, v, sinks, window):
    b, s, n = q.shape
    pairs = GQA_KV_HEADS // 2
    qw = n // pairs
    tq = Q_TILE
    span = tq + 2 * WINDOW if window else s
    return pl.pallas_call(
        functools.partial(_gqa_pair_body, window),
        grid=(b, pairs, s // tq),
        in_specs=[
            pl.BlockSpec((None, tq, qw), lambda bi, p, i: (bi, i, p)),
            pl.BlockSpec((None, s, LANES), lambda bi, p, i: (bi, 0, p)),
            pl.BlockSpec((None, s, LANES), lambda bi, p, i: (bi, 0, p)),
            pl.BlockSpec(memory_space=pltpu.SMEM),
        ],
        out_specs=pl.BlockSpec((None, tq, qw), lambda bi, p, i: (bi, i, p)),
        out_shape=jax.ShapeDtypeStruct((b, s, n), BF16),
        scratch_shapes=_attn_scratch(tq, span, s),
        compiler_params=_params("parallel", "parallel", "arbitrary"),
        name="window_attn" if window else "axial_attn",
    )(q, k, v, sinks)


def _proj_out_body(o_ref, wo_ref, x_ref, g_ref, wr_ref, xn_ref, hp_ref, aff_ref, x2_ref):
    x2_ref[...] = x_ref[...].reshape(x2_ref.shape)
    xn = x2_ref[...] + jnp.dot(o_ref[...], wo_ref[...], preferred_element_type=F32)
    xn_ref[...] = xn.reshape(xn_ref.shape)
    ms = jnp.mean(xn * xn, axis=-1, keepdims=True)
    h = xn * lax.rsqrt(ms + RMS_EPS) * g_ref[...]
    half = h.shape[1] // 2
    hi = lax.bitcast_convert_type(h[:, :half].astype(BF16).astype(F32), jnp.uint32)
    lo = lax.bitcast_convert_type(h[:, half:].astype(BF16).astype(F32), jnp.uint32)
    hp_ref[...] = (hi | (lo >> 16)).reshape(hp_ref.shape)
    logits = lax.dot_general(wr_ref[...], h, NT_DIMS, preferred_element_type=F32,
                             precision=lax.Precision.HIGHEST)
    m = jnp.max(logits, axis=0, keepdims=True)
    e = jnp.exp(logits - m)
    aff_ref[...] = e / jnp.sum(e, axis=0, keepdims=True)


def _proj_out(o, wo, x, g, wr_t, seq):
    t, _, d = x.shape
    tm = TOKEN_BLOCK
    e = wr_t.shape[0]
    sblocks = seq // tm
    row = lambda i: (i, 0)
    fixed = lambda i: (0, 0)
    return pl.pallas_call(
        _proj_out_body,
        grid=(t // tm,),
        in_specs=[
            pl.BlockSpec((tm, o.shape[1]), row),
            pl.BlockSpec(wo.shape, fixed),
            pl.BlockSpec((tm, 1, d), lambda i: (i, 0, 0)),
            pl.BlockSpec((1, d), fixed),
            pl.BlockSpec((e, d), fixed),
        ],
        out_specs=[
            pl.BlockSpec((tm, 1, d), lambda i: (i, 0, 0)),
            pl.BlockSpec((tm, 1, d // 2), lambda i: (i, 0, 0)),
            pl.BlockSpec((None, e, tm), lambda i: (i // sblocks, 0, i % sblocks)),
        ],
        out_shape=[
            jax.ShapeDtypeStruct((t, 1, d), F32),
            jax.ShapeDtypeStruct((t, 1, d // 2), jnp.uint32),
            jax.ShapeDtypeStruct((t // seq, e, seq), F32),
        ],
        scratch_shapes=[pltpu.VMEM((tm, d), F32)],
        compiler_params=_params("parallel"),
        name="proj_out_router",
    )(o, wo, x, g, wr_t)


PREFIX_CHUNK = 256


def _exclusive_prefix(mask_f):
    e, s = mask_f.shape
    r = lax.broadcasted_iota(I32, (PREFIX_CHUNK, PREFIX_CHUNK), 0)
    c = lax.broadcasted_iota(I32, (PREFIX_CHUNK, PREFIX_CHUNK), 1)
    upper = jnp.where(r < c, 1.0, 0.0).astype(BF16)
    offset = jnp.zeros((e, 1), F32)
    outs = []
    for j in range(s // PREFIX_CHUNK):
        mj = mask_f[:, j * PREFIX_CHUNK:(j + 1) * PREFIX_CHUNK]
        outs.append(jnp.dot(mj.astype(BF16), upper, preferred_element_type=F32) + offset)
        offset = offset + jnp.sum(mj, axis=-1, keepdims=True)
    return jnp.concatenate(outs, axis=1)


def _route_body(cap, aff_ref, out_ref, slot_ref):
    aff = aff_ref[...]
    e, s = aff.shape
    bits = lax.bitcast_convert_type(aff, I32)

    def count(mask):
        return jnp.sum(jnp.where(mask, 1.0, 0.0), axis=-1, keepdims=True)

    def search(i, thr):
        cand = thr | lax.shift_left(jnp.int32(1), 30 - i)
        return jnp.where(count(bits >= cand) >= cap, cand, thr)

    thr = lax.fori_loop(0, 31, search, jnp.zeros((e, 1), I32))
    gt = bits > thr
    eq = bits == thr
    need = cap - count(gt)
    eq_rank = _exclusive_prefix(jnp.where(eq, 1.0, 0.0))
    sel = gt | (eq & (eq_rank < need))
    sel_f = jnp.where(sel, 1.0, 0.0)
    slot_ref[...] = jnp.where(sel, _exclusive_prefix(sel_f), -1.0)

    slot_id = lax.broadcasted_iota(I32, (cap, s), 0).astype(F32)
    token = lax.broadcasted_iota(I32, (1, s), 1).astype(F32)
    col = lax.broadcasted_iota(I32, (cap, out_ref.shape[2]), 1)
    expert = lax.broadcasted_iota(I32, (e, s), 0)

    def per_expert(ei, carry):
        mine = expert == ei
        slot_row = jnp.sum(jnp.where(mine, slot_ref[...], 0.0), axis=0, keepdims=True)
        aff_row = jnp.sum(jnp.where(mine, aff_ref[...], 0.0), axis=0, keepdims=True)
        hit = slot_row == slot_id
        idx = jnp.sum(jnp.where(hit, token, 0.0), axis=-1, keepdims=True)
        gate = jnp.sum(jnp.where(hit, aff_row, 0.0), axis=-1, keepdims=True)
        out_ref[ei] = jnp.where(col == 0, idx, gate)
        return carry

    lax.fori_loop(0, e, per_expert, 0)


ROUTE_COLS = 8


def _route(aff, cap):
    b, e, s = aff.shape
    return pl.pallas_call(
        functools.partial(_route_body, cap),
        grid=(b,),
        in_specs=[pl.BlockSpec((None, e, s), lambda bi: (bi, 0, 0))],
        out_specs=pl.BlockSpec((None, e, cap, ROUTE_COLS), lambda bi: (bi, 0, 0, 0)),
        out_shape=jax.ShapeDtypeStruct((b, e, cap, ROUTE_COLS), F32),
        scratch_shapes=[pltpu.VMEM((e, s), F32)],
        compiler_params=_params("parallel"),
        name="ec_route",
    )(aff)


ROW_UNROLL = 8


def _gather_body(idx_ref, h_ref, o_ref):
    n = o_ref.shape[0]

    def rows(j, carry):
        base = j * ROW_UNROLL
        for u in range(ROW_UNROLL):
            o_ref[base + u] = h_ref[idx_ref[0, base + u]]
        return carry

    lax.fori_loop(0, n // ROW_UNROLL, rows, 0)


def _gather_rows(idx, hp):
    b, s, _, w = hp.shape
    n = idx.shape[2]
    return pl.pallas_call(
        _gather_body,
        grid=(b,),
        in_specs=[
            pl.BlockSpec((None, 1, n), lambda bi: (bi, 0, 0), memory_space=pltpu.SMEM),
            pl.BlockSpec((None, s, 1, w), lambda bi: (bi, 0, 0, 0)),
        ],
        out_specs=pl.BlockSpec((None, n, 1, w), lambda bi: (bi, 0, 0, 0)),
        out_shape=jax.ShapeDtypeStruct((b, n, 1, w), hp.dtype),
        compiler_params=_params("parallel"),
        name="ec_gather",
    )(idx, hp)


def _expert_body(xp_ref, gate_ref, wg_ref, wu_ref, wd_ref, y_ref, xp2_ref):
    xp2_ref[...] = xp_ref[...].reshape(xp2_ref.shape)
    xp = xp2_ref[...]
    x_hi = lax.bitcast_convert_type(xp & jnp.uint32(0xFFFF0000), F32).astype(BF16)
    x_lo = lax.bitcast_convert_type(xp << 16, F32).astype(BF16)
    xe = jnp.concatenate([x_hi, x_lo], axis=1)
    hg = jnp.dot(xe, wg_ref[...], preferred_element_type=F32)
    hu = jnp.dot(xe, wu_ref[...], preferred_element_type=F32)
    hid = (hg * jax.nn.sigmoid(hg) * hu).astype(BF16)
    y = jnp.dot(hid, wd_ref[...], preferred_element_type=F32)
    y_ref[...] = (y * gate_ref[:, 1:2]).reshape(y_ref.shape)


def _experts(xg, route, wg, wu, wd):
    b, n, _, half = xg.shape
    e, d, f = wg.shape
    cap = n // e
    return pl.pallas_call(
        _expert_body,
        grid=(e, b),
        in_specs=[
            pl.BlockSpec((None, cap, 1, half), lambda ei, bi: (bi, ei, 0, 0)),
            pl.BlockSpec((None, None, cap, ROUTE_COLS), lambda ei, bi: (bi, ei, 0, 0)),
            pl.BlockSpec((None, d, f), lambda ei, bi: (ei, 0, 0)),
            pl.BlockSpec((None, d, f), lambda ei, bi: (ei, 0, 0)),
            pl.BlockSpec((None, f, d), lambda ei, bi: (ei, 0, 0)),
        ],
        out_specs=pl.BlockSpec((None, cap, 1, d), lambda ei, bi: (bi, ei, 0, 0)),
        out_shape=jax.ShapeDtypeStruct((b, n, 1, d), F32),
        scratch_shapes=[pltpu.VMEM((cap, half), jnp.uint32)],
        compiler_params=_params("parallel", "parallel"),
        name="ec_experts",
    )(xg, route, wg, wu, wd)


def _combine_body(idx_ref, x_ref, y_ref, o_ref):
    @pl.when(pl.program_id(1) == 0)
    def _():
        o_ref[...] = x_ref[...]

    def rows(j, carry):
        base = j * ROW_UNROLL
        sums = [o_ref[idx_ref[0, base + u]] + y_ref[base + u] for u in range(ROW_UNROLL)]
        for u in range(ROW_UNROLL):
            o_ref[idx_ref[0, base + u]] = sums[u]
        return carry

    lax.fori_loop(0, y_ref.shape[0] // ROW_UNROLL, rows, 0)


def _combine(idx, x, y, cap):
    b, s, _, d = x.shape
    e = y.shape[1] // cap
    assert cap % ROW_UNROLL == 0
    return pl.pallas_call(
        _combine_body,
        grid=(b, e),
        in_specs=[
            pl.BlockSpec((None, 1, cap), lambda bi, ei: (bi, 0, ei), memory_space=pltpu.SMEM),
            pl.BlockSpec((None, s, 1, d), lambda bi, ei: (bi, 0, 0, 0)),
            pl.BlockSpec((None, cap, 1, d), lambda bi, ei: (bi, ei, 0, 0)),
        ],
        out_specs=pl.BlockSpec((None, s, 1, d), lambda bi, ei: (bi, 0, 0, 0)),
        out_shape=jax.ShapeDtypeStruct(x.shape, F32),
        compiler_params=_params("parallel", "arbitrary"),
        name="ec_combine",
    )(idx, x, y)


def _final_norm_body(x_ref, g_ref, o_ref):
    o_ref[...] = x_ref[...].reshape(o_ref.shape)
    x = o_ref[...]
    ms = jnp.mean(x * x, axis=-1, keepdims=True)
    o_ref[...] = x * lax.rsqrt(ms + RMS_EPS) * g_ref[...]


def _final_norm(x, g):
    t, _, d = x.shape
    tm = TOKEN_BLOCK
    return pl.pallas_call(
        _final_norm_body,
        grid=(t // tm,),
        in_specs=[pl.BlockSpec((tm, 1, d), lambda i: (i, 0, 0)), pl.BlockSpec((1, d), lambda i: (0, 0))],
        out_specs=pl.BlockSpec((tm, d), lambda i: (i, 0)),
        out_shape=jax.ShapeDtypeStruct((t, d), F32),
        compiler_params=_params("parallel"),
        name="final_norm",
    )(x, g)


def kernel(x, attn_norm_g, ffn_norm_g, final_norm_g, a_w_in, a_w_o, a_lambda_q1, a_lambda_k1, a_lambda_q2,
           a_lambda_k2, a_subln_g, b_w_in, b_w_o, b_q_norm_g, b_k_norm_g, c_w_in, c_w_o, c_sinks,
           moe_w_router, moe_w_gate, moe_w_up, moe_w_down):
    batch, seq, d = x.shape
    depth = attn_norm_g.shape[0]
    t = batch * seq
    cap = EC_CAPACITY_FACTOR * seq // N_EXPERTS
    gqa_q = GQA_HEADS * HEAD_DIM
    gqa_kv = GQA_KV_HEADS * HEAD_DIM

    tables_1d = _lane_tables_1d(seq)
    tables_ax = _lane_tables_axial(seq)
    ones = jnp.ones((1, LANES), F32)

    xt = x.reshape(t, 1, d)
    for i in range(depth):
        m, j = i % N_MIXERS, i // N_MIXERS
        g_attn = attn_norm_g[i].reshape(1, d)
        if m == 0:
            w = a_w_in[j].astype(BF16)
            q, k, v = _proj_in(0, xt, g_attn, w[:, :d], w[:, d:2 * d], w[:, 2 * d:], *tables_1d, ones, ones, seq)
            lambda_init = 0.8 - 0.6 * math.exp(-0.3 * i)
            vec = lambda a: a[j].reshape(1, HEAD_DIM)
            o = _diff_attn(q.reshape(batch, seq, d), k.reshape(batch, seq, d), v.reshape(batch, seq, d),
                           vec(a_lambda_q1), vec(a_lambda_k1), vec(a_lambda_q2), vec(a_lambda_k2),
                           a_subln_g[j].reshape(1, 2 * HEAD_DIM), lambda_init)
            w_o = a_w_o[j]
        else:
            w_in = b_w_in[j] if m == 1 else c_w_in[j]
            w = w_in.astype(BF16)
            if m == 1:
                tables = tables_ax
                qg = jnp.tile(b_q_norm_g[j], 2).reshape(1, LANES)
                kg = jnp.tile(b_k_norm_g[j], 2).reshape(1, LANES)
                sinks = jnp.zeros((GQA_HEADS,), F32)
            else:
                tables, qg, kg, sinks = tables_1d, ones, ones, c_sinks[j]
            q, k, v = _proj_in(m, xt, g_attn, w[:, :gqa_q], w[:, gqa_q:gqa_q + gqa_kv], w[:, gqa_q + gqa_kv:],
                               *tables, qg, kg, seq)
            o = _gqa_attn(q.reshape(batch, seq, gqa_q), k.reshape(batch, seq, gqa_kv),
                          v.reshape(batch, seq, gqa_kv), sinks, window=(m == 2))
            w_o = b_w_o[j] if m == 1 else c_w_o[j]

        xn, hp, aff = _proj_out(o.reshape(t, -1), w_o.astype(BF16), xt, ffn_norm_g[i].reshape(1, d),
                                moe_w_router[i].T, seq)
        route = _route(aff, cap)
        idx = route[..., 0].astype(I32).reshape(batch, 1, N_EXPERTS * cap)
        xg = _gather_rows(idx, hp.reshape(batch, seq, 1, d // 2))
        y = _experts(xg, route, moe_w_gate[i].astype(BF16),
                     moe_w_up[i].astype(BF16), moe_w_down[i].astype(BF16))
        xo = _combine(idx, xn.reshape(batch, seq, 1, d), y, cap)
        xt = xo.reshape(t, 1, d)
    return _final_norm(xt, final_norm_g.reshape(1, d)).reshape(batch, seq, d)
```

```python
import functools
import math

import jax
import jax.numpy as jnp
from jax import lax
from jax.experimental import pallas as pl
from jax.experimental.pallas import tpu as pltpu

F32 = jnp.float32
BF16 = jnp.bfloat16
I32 = jnp.int32

D_MODEL = 1024
HEAD_DIM = 64
GQA_HEADS = 16
GQA_KV_HEADS = 4
GRID_W = 64
WINDOW = 128
ROPE_THETA = 10000.0
N_EXPERTS = 16
EC_CAPACITY_FACTOR = 2
RMS_EPS = 1e-6
N_MIXERS = 3

LANES = 128
VMEM_LIMIT = 56 * 1024 * 1024

TOKEN_BLOCK = 512
Q_TILE = 256
NT_DIMS = (((1,), (1,)), ((), ()))


def _params(*sem):
    return pltpu.CompilerParams(dimension_semantics=sem, vmem_limit_bytes=VMEM_LIMIT)


def _rope_tables(pos, dim):
    inv = ROPE_THETA ** (-jnp.arange(0, dim, 2, dtype=F32) / dim)
    ang = pos.astype(F32)[:, None] * inv[None, :]
    return jnp.cos(ang), jnp.sin(ang)


def _lane_tables_1d(seq):
    cos, sin = _rope_tables(jnp.arange(seq), HEAD_DIM)
    cos_l = jnp.tile(cos, (1, LANES // 32))
    sign = jnp.where((jnp.arange(LANES) & 32) == 0, -1.0, 1.0).astype(F32)
    sin_l = jnp.tile(sin, (1, LANES // 32)) * sign[None, :]
    return cos_l, sin_l


def _lane_tables_axial(seq):
    rows = seq // GRID_W
    row_idx = jnp.repeat(jnp.arange(rows), GRID_W)
    col_idx = jnp.tile(jnp.arange(GRID_W), rows)
    rc, rs = _rope_tables(row_idx, HEAD_DIM // 2)
    cc, cs = _rope_tables(col_idx, HEAD_DIM // 2)
    cos64 = jnp.concatenate([rc, rc, cc, cc], axis=1)
    sin64 = jnp.concatenate([-rs, rs, -cs, cs], axis=1)
    return jnp.tile(cos64, (1, 2)), jnp.tile(sin64, (1, 2))


def _rope_chunk(c, cos, sin, half):
    lane = lax.broadcasted_iota(I32, c.shape, 1)
    low = (lane & half) == 0
    partner = jnp.where(low, pltpu.roll(c, LANES - half, 1), pltpu.roll(c, half, 1))
    return c * cos + partner * sin


def _head_rms_chunk(c, gain):
    lane = lax.broadcasted_iota(I32, c.shape, 1)
    low = lane < HEAD_DIM
    cc = c * c
    s_lo = jnp.sum(jnp.where(low, cc, 0.0), axis=-1, keepdims=True)
    s_hi = jnp.sum(jnp.where(low, 0.0, cc), axis=-1, keepdims=True)
    ms = jnp.where(low, s_lo, s_hi) * (1.0 / HEAD_DIM)
    return c * lax.rsqrt(ms + RMS_EPS) * gain


def _load_residual(x_ref, x2_ref):
    if len(x_ref.shape) == 2:
        return x_ref[...]
    x2_ref[...] = x_ref[...].reshape(x2_ref.shape)
    return x2_ref[...]


def _residual_spec(x, tm):
    if x.ndim == 2:
        return pl.BlockSpec((tm, x.shape[-1]), lambda i: (i, 0))
    return pl.BlockSpec((tm, 1, x.shape[-1]), lambda i: (i, 0, 0))


def _proj_in_body(mixer, x_ref, g_ref, wq_ref, wk_ref, wv_ref, cos_ref, sin_ref, qg_ref, kg_ref,
                  q_ref, k_ref, v_ref, x2_ref):
    x = _load_residual(x_ref, x2_ref)
    ms = jnp.mean(x * x, axis=-1, keepdims=True)
    h = (x * lax.rsqrt(ms + RMS_EPS) * g_ref[...]).astype(BF16)
    cos = cos_ref[...]
    sin = sin_ref[...]
    half = 16 if mixer == 1 else 32

    def finish(acc, out_ref, scale, gain_ref):
        for j in range(acc.shape[1] // LANES):
            c = acc[:, j * LANES:(j + 1) * LANES]
            if mixer == 1:
                c = _head_rms_chunk(c, gain_ref[...])
            c = _rope_chunk(c, cos, sin, half)
            if scale != 1.0:
                c = c * scale
            out_ref[:, j * LANES:(j + 1) * LANES] = c.astype(BF16)

    finish(jnp.dot(h, wq_ref[...], preferred_element_type=F32), q_ref, Q_SCALE, qg_ref)
    finish(jnp.dot(h, wk_ref[...], preferred_element_type=F32), k_ref, 1.0, kg_ref)
    v_ref[...] = jnp.dot(h, wv_ref[...], preferred_element_type=F32).astype(BF16)


def _proj_in(mixer, x, g, wq, wk, wv, cos_l, sin_l, qg, kg, seq):
    t, d = x.shape[0], x.shape[-1]
    tm = TOKEN_BLOCK
    nq, nk, nv = wq.shape[1], wk.shape[1], wv.shape[1]
    sblocks = seq // tm
    row = lambda i: (i, 0)
    fixed = lambda i: (0, 0)
    tab = lambda i: (i % sblocks, 0)
    return pl.pallas_call(
        functools.partial(_proj_in_body, mixer),
        grid=(t // tm,),
        in_specs=[
            _residual_spec(x, tm),
            pl.BlockSpec((1, d), fixed),
            pl.BlockSpec((d, nq), fixed),
            pl.BlockSpec((d, nk), fixed),
            pl.BlockSpec((d, nv), fixed),
            pl.BlockSpec((tm, LANES), tab),
            pl.BlockSpec((tm, LANES), tab),
            pl.BlockSpec((1, LANES), fixed),
            pl.BlockSpec((1, LANES), fixed),
        ],
        out_specs=[
            pl.BlockSpec((tm, nq), row),
            pl.BlockSpec((tm, nk), row),
            pl.BlockSpec((tm, nv), row),
        ],
        out_shape=[
            jax.ShapeDtypeStruct((t, nq), BF16),
            jax.ShapeDtypeStruct((t, nk), BF16),
            jax.ShapeDtypeStruct((t, nv), BF16),
        ],
        scratch_shapes=[pltpu.VMEM((tm, d), F32)],
        compiler_params=_params("parallel"),
        name=f"proj_in_m{mixer}",
    )(x, g, wq, wk, wv, cos_l, sin_l, qg, kg)


LOG2E = 1.4426950408889634
Q_SCALE = HEAD_DIM ** -0.5 * LOG2E
KEY_CHUNK = 512


def _attention_chains(n_chains, q_of, k_of, v_aug_of, finish, s_ref, p_ref, span, mask_of=None, floor_of=None):
    kc = min(KEY_CHUNK, span)
    pieces = span // kc

    def scores(c, j, q):
        s = lax.dot_general(q, k_of(c, j), NT_DIMS, preferred_element_type=F32)
        if mask_of is not None:
            s = jnp.where(mask_of(j), s, -jnp.inf)
        s_ref[c % 2, :, j * kc:(j + 1) * kc] = s

    q = q_of(0)
    for j in range(pieces):
        scores(0, j, q)
    for c in range(n_chains):
        slot = c % 2
        m = jnp.max(s_ref[slot], axis=-1, keepdims=True)
        if floor_of is not None:
            m = jnp.maximum(m, floor_of(c))
        q = q_of(c + 1) if c + 1 < n_chains else None
        for j in range(pieces):
            if q is not None:
                scores(c + 1, j, q)
            e = jnp.exp2(s_ref[slot, :, j * kc:(j + 1) * kc] - m)
            p_ref[slot, :, j * kc:(j + 1) * kc] = e.astype(BF16)
        finish(c, jnp.dot(p_ref[slot], v_aug_of(c), preferred_element_type=F32), m)


def _fill_v_aug(vaug_ref, v_ref):
    vaug_ref[:, :LANES] = v_ref[...]
    vaug_ref[:, LANES:] = jnp.ones((vaug_ref.shape[0], LANES), BF16)


def _lane_tiles(e):
    return [e[:, j * LANES:(j + 1) * LANES] for j in range(e.shape[1] // LANES)]


def _tree(op, parts):
    while len(parts) > 1:
        parts = [op(a, b) for a, b in zip(parts[::2], parts[1::2])]
    return parts[0]


def _lane_tile_sum(e):
    return _tree(jnp.add, _lane_tiles(e))


def _lane_tile_max(pieces):
    return _tree(jnp.maximum, [_tree(jnp.maximum, _lane_tiles(p)) for p in pieces])


def _diff_attn_body(lambda_init, q_ref, k_ref, v_ref, lq1_ref, lk1_ref, lq2_ref, lk2_ref, sg_ref, o_ref,
                    s00, s01, s10, s11, p0, p1, linv0, linv1, acc_ref):
    s_ref, p_ref, linv_ref = ((s00, s01), (s10, s11)), (p0, p1), (linv0, linv1)
    tq = s00.shape[0]
    span = k_ref.shape[0]
    kc = min(KEY_CHUNK, span)
    pieces = span // kc
    tiles = q_ref.shape[0] // tq
    lane = lax.broadcasted_iota(I32, (tq, LANES), 1)
    lam = (jnp.exp(jnp.sum(lq1_ref[...] * lk1_ref[...], axis=-1, keepdims=True))
           - jnp.exp(jnp.sum(lq2_ref[...] * lk2_ref[...], axis=-1, keepdims=True))
           + lambda_init)

    def q_of(t, mp):
        q = q_ref[t * tq:(t + 1) * tq, :]
        keep = lane < HEAD_DIM if mp == 0 else lane >= HEAD_DIM
        return jnp.where(keep, q, jnp.zeros_like(q))

    def scores(t, mp, j, q):
        s_ref[t % 2][mp][:, j * kc:(j + 1) * kc] = lax.dot_general(
            q, k_ref[j * kc:(j + 1) * kc, :], NT_DIMS, preferred_element_type=F32)

    qs = [q_of(0, 0), q_of(0, 1)]
    for j in range(pieces):
        scores(0, 0, j, qs[0])
        scores(0, 1, j, qs[1])
    blocks = tq // SOFTMAX_ROWS
    cols = [slice(j * kc, (j + 1) * kc) for j in range(pieces)]

    def softmax_block(t, r):
        slot = t % 2
        rows = slice(r * SOFTMAX_ROWS, (r + 1) * SOFTMAX_ROWS)
        sums = []
        for mp in range(2):
            tile_max = _lane_tile_max([s_ref[slot][mp][rows, c] for c in cols])
            m = jnp.max(tile_max, axis=-1, keepdims=True)
            part = jnp.zeros((SOFTMAX_ROWS, LANES), F32)
            for c in cols:
                e = jnp.exp2(s_ref[slot][mp][rows, c] - m)
                s_ref[slot][mp][rows, c] = e
                part = part + _lane_tile_sum(e)
            sums.append(jnp.sum(part, axis=-1, keepdims=True))
        ratio = lam * sums[0] / sums[1]
        for c in cols:
            a = s_ref[slot][0][rows, c] - s_ref[slot][1][rows, c] * ratio
            p_ref[slot][rows, c] = a.astype(BF16)
        linv_ref[slot][rows, :] = jnp.broadcast_to(1.0 / sums[0], (SOFTMAX_ROWS, LANES))

    def finish(t):
        o = acc_ref[...] * linv_ref[t % 2][...]
        ms = jnp.mean(o * o, axis=-1, keepdims=True)
        o = o * lax.rsqrt(ms + RMS_EPS) * sg_ref[...] * (1.0 - lambda_init)
        o_ref[t * tq:(t + 1) * tq, :] = o.astype(BF16)

    for t in range(tiles + 1):
        ahead = []
        if t + 1 < tiles:
            qs = [q_of(t + 1, 0), q_of(t + 1, 1)]
            ahead = [(mp, j) for mp in range(2) for j in range(pieces)]
        per_block = -(-len(ahead) // blocks)
        for r in range(max(blocks, pieces)):
            for mp, j in ahead[r * per_block:(r + 1) * per_block]:
                scores(t + 1, mp, j, qs[mp])
            if t >= 1 and r < pieces:
                part_o = jnp.dot(p_ref[(t - 1) % 2][:, cols[r]], v_ref[cols[r], :], preferred_element_type=F32)
                acc_ref[...] = part_o if r == 0 else acc_ref[...] + part_o
            if t < tiles and r < blocks:
                softmax_block(t, r)
        if t >= 1:
            finish(t - 1)


DIFF_Q_TILES = 4
SOFTMAX_ROWS = 32


def _attn_scratch(tq, span, keys):
    return [pltpu.VMEM((keys, 2 * LANES), BF16), pltpu.VMEM((2, tq, span), F32), pltpu.VMEM((2, tq, span), BF16)]


def _diff_attn(q, k, v, lq1, lk1, lq2, lk2, subln_g, lambda_init):
    b, s, n = q.shape
    heads = n // LANES
    tq = Q_TILE
    rows = DIFF_Q_TILES * tq
    vec = pl.BlockSpec((1, HEAD_DIM), lambda bi, h, i: (0, 0))
    return pl.pallas_call(
        functools.partial(_diff_attn_body, lambda_init),
        grid=(b, heads, s // rows),
        in_specs=[
            pl.BlockSpec((None, rows, LANES), lambda bi, h, i: (bi, i, h)),
            pl.BlockSpec((None, s, LANES), lambda bi, h, i: (bi, 0, h)),
            pl.BlockSpec((None, s, LANES), lambda bi, h, i: (bi, 0, h)),
            vec, vec, vec, vec,
            pl.BlockSpec((1, LANES), lambda bi, h, i: (0, 0)),
        ],
        out_specs=pl.BlockSpec((None, rows, LANES), lambda bi, h, i: (bi, i, h)),
        out_shape=jax.ShapeDtypeStruct((b, s, n), BF16),
        scratch_shapes=([pltpu.VMEM((tq, s), F32)] * 4 + [pltpu.VMEM((tq, s), BF16)] * 2
                        + [pltpu.VMEM((tq, LANES), F32)] * 3),
        compiler_params=_params("parallel", "parallel", "parallel"),
        name="diff_attn",
    )(q, k, v, lq1, lk1, lq2, lk2, subln_g)


def _place_head(x, src_half, dst_half):
    return x if src_half == dst_half else pltpu.roll(x, HEAD_DIM, 1)


def _gqa_pair_body(window, q_ref, k_ref, v_ref, sink_ref, o_ref, vaug_ref, s_ref, p_ref):
    pl.when(pl.program_id(2) == 0)(functools.partial(_fill_v_aug, vaug_ref, v_ref))
    tq = q_ref.shape[0]
    span = s_ref.shape[2]
    kc = min(KEY_CHUNK, span)
    group = GQA_HEADS // GQA_KV_HEADS
    pair = pl.program_id(1)
    lane = lax.broadcasted_iota(I32, (tq, LANES), 1)
    if window:
        i = pl.program_id(2)
        start = pl.multiple_of(jnp.clip(i * tq - WINDOW, 0, k_ref.shape[0] - span), LANES)
        qpos = i * tq + lax.broadcasted_iota(I32, (tq, kc), 0)
        kpos = start + lax.broadcasted_iota(I32, (tq, kc), 1)
        mask_of = lambda j: jnp.abs(qpos - (kpos + j * kc)) <= WINDOW
        k_of = lambda c, j: k_ref[pl.ds(start + j * kc, kc), :]
        v_aug_of = lambda c: vaug_ref[pl.ds(start, span), :]
        sink2 = lambda c: sink_ref[pair * 2 * group + c] * LOG2E
    else:
        mask_of = sink2 = None
        k_of = lambda c, j: k_ref[j * kc:(j + 1) * kc, :]
        v_aug_of = lambda c: vaug_ref[...]

    def q_of(c):
        hh, kvl = c % 2, c // group
        qc = q_ref[:, (c // 2) * LANES:(c // 2 + 1) * LANES]
        qh = jnp.where((lane >= hh * HEAD_DIM) & (lane < (hh + 1) * HEAD_DIM), qc, jnp.zeros_like(qc))
        return qh if hh == kvl else _place_head(qh.astype(F32), hh, kvl).astype(BF16)

    first = []

    def finish(c, o_aug, m):
        hh, kvl = c % 2, c // group
        total = o_aug[:, LANES:LANES + 1]
        if window:
            total = total + jnp.exp2(sink2(c) - m)
        o = _place_head(o_aug[:, :LANES] * (1.0 / total), kvl, hh)
        if hh == 0:
            first.append(o)
        else:
            o_ref[:, (c // 2) * LANES:(c // 2 + 1) * LANES] = jnp.where(lane < HEAD_DIM, first.pop(), o).astype(BF16)

    _attention_chains(2 * group, q_of, k_of, v_aug_of, finish, s_ref, p_ref, span, mask_of, sink2)


def _gqa_attn(q, k, v, sinks, window):
    b, s, n = q.shape
    pairs = GQA_KV_HEADS // 2
    qw = n // pairs
    tq = Q_TILE
    span = tq + 2 * WINDOW if window else s
    return pl.pallas_call(
        functools.partial(_gqa_pair_body, window),
        grid=(b, pairs, s // tq),
        in_specs=[
            pl.BlockSpec((None, tq, qw), lambda bi, p, i: (bi, i, p)),
            pl.BlockSpec((None, s, LANES), lambda bi, p, i: (bi, 0, p)),
            pl.BlockSpec((None, s, LANES), lambda bi, p, i: (bi, 0, p)),
            pl.BlockSpec(memory_space=pltpu.SMEM),
        ],
        out_specs=pl.BlockSpec((None, tq, qw), lambda bi, p, i: (bi, i, p)),
        out_shape=jax.ShapeDtypeStruct((b, s, n), BF16),
        scratch_shapes=_attn_scratch(tq, span, s),
        compiler_params=_params("parallel", "parallel", "arbitrary"),
        name="window_attn" if window else "axial_attn",
    )(q, k, v, sinks)


def _store_rows(ref, val):
    ref[...] = val.reshape(ref.shape)


def _proj_out_body(o_ref, wo_ref, x_ref, g_ref, wr_ref, xn_ref, hp_ref, aff_ref, x2_ref):
    xn = _load_residual(x_ref, x2_ref) + jnp.dot(o_ref[...], wo_ref[...], preferred_element_type=F32)
    _store_rows(xn_ref, xn)
    ms = jnp.mean(xn * xn, axis=-1, keepdims=True)
    h = xn * lax.rsqrt(ms + RMS_EPS) * g_ref[...]
    hb = h.astype(BF16)
    hb32 = hb.astype(F32)
    half = h.shape[1] // 2
    hi = lax.bitcast_convert_type(hb32[:, :half], jnp.uint32)
    lo = lax.bitcast_convert_type(hb32[:, half:], jnp.uint32)
    _store_rows(hp_ref, hi | (lo >> 16))
    wr = wr_ref[...]
    wr_hi = wr.astype(BF16)
    wr_lo = (wr - wr_hi.astype(F32)).astype(BF16)
    h_lo = (h - hb32).astype(BF16)
    logits = (lax.dot_general(wr_hi, hb, NT_DIMS, preferred_element_type=F32)
              + lax.dot_general(wr_lo, hb, NT_DIMS, preferred_element_type=F32)
              + lax.dot_general(wr_hi, h_lo, NT_DIMS, preferred_element_type=F32))
    m = jnp.max(logits, axis=0, keepdims=True)
    e = jnp.exp(logits - m)
    aff_ref[...] = e / jnp.sum(e, axis=0, keepdims=True)


def _proj_out(o, wo, x, g, wr_t, seq):
    t, d = x.shape[0], x.shape[-1]
    tm = TOKEN_BLOCK
    e = wr_t.shape[0]
    sblocks = seq // tm
    row = lambda i: (i, 0)
    fixed = lambda i: (0, 0)
    return pl.pallas_call(
        _proj_out_body,
        grid=(t // tm,),
        in_specs=[
            pl.BlockSpec((tm, o.shape[1]), row),
            pl.BlockSpec(wo.shape, fixed),
            _residual_spec(x, tm),
            pl.BlockSpec((1, d), fixed),
            pl.BlockSpec((e, d), fixed),
        ],
        out_specs=[
            pl.BlockSpec((tm, 1, d), lambda i: (i, 0, 0)),
            pl.BlockSpec((tm, 1, d // 2), lambda i: (i, 0, 0)),
            pl.BlockSpec((None, e, tm), lambda i: (i // sblocks, 0, i % sblocks)),
        ],
        out_shape=[
            jax.ShapeDtypeStruct((t, 1, d), F32),
            jax.ShapeDtypeStruct((t, 1, d // 2), jnp.uint32),
            jax.ShapeDtypeStruct((t // seq, e, seq), F32),
        ],
        scratch_shapes=[pltpu.VMEM((tm, d), F32)],
        compiler_params=_params("parallel"),
        name="proj_out_router",
    )(o, wo, x, g, wr_t)


PREFIX_CHUNK = 256


def _exclusive_prefix(mask_f):
    e, s = mask_f.shape
    r = lax.broadcasted_iota(I32, (PREFIX_CHUNK, PREFIX_CHUNK), 0)
    c = lax.broadcasted_iota(I32, (PREFIX_CHUNK, PREFIX_CHUNK), 1)
    upper = jnp.where(r < c, 1.0, 0.0).astype(BF16)
    offset = jnp.zeros((e, 1), F32)
    outs = []
    for j in range(s // PREFIX_CHUNK):
        mj = mask_f[:, j * PREFIX_CHUNK:(j + 1) * PREFIX_CHUNK]
        outs.append(jnp.dot(mj.astype(BF16), upper, preferred_element_type=F32) + offset)
        offset = offset + jnp.sum(mj, axis=-1, keepdims=True)
    return jnp.concatenate(outs, axis=1)


def _route_body(cap, aff_ref, out_ref, slot_ref):
    aff = aff_ref[...]
    e, s = aff.shape
    bits = lax.bitcast_convert_type(aff, I32)

    def count(mask):
        return jnp.sum(jnp.where(mask, 1.0, 0.0), axis=-1, keepdims=True)

    def search(i, thr):
        cand = thr | lax.shift_left(jnp.int32(1), 30 - i)
        return jnp.where(count(bits >= cand) >= cap, cand, thr)

    thr = lax.fori_loop(0, 31, search, jnp.zeros((e, 1), I32))
    gt = bits > thr
    eq = bits == thr
    need = cap - count(gt)
    eq_rank = _exclusive_prefix(jnp.where(eq, 1.0, 0.0))
    sel = gt | (eq & (eq_rank < need))
    sel_f = jnp.where(sel, 1.0, 0.0)
    slot_ref[...] = jnp.where(sel, _exclusive_prefix(sel_f), -1.0)

    slot_id = lax.broadcasted_iota(I32, (cap, s), 0).astype(F32)
    token = lax.broadcasted_iota(I32, (1, s), 1).astype(F32)
    col = lax.broadcasted_iota(I32, (cap, out_ref.shape[2]), 1)
    expert = lax.broadcasted_iota(I32, (e, s), 0)

    def per_expert(ei, carry):
        mine = expert == ei
        slot_row = jnp.sum(jnp.where(mine, slot_ref[...], 0.0), axis=0, keepdims=True)
        aff_row = jnp.sum(jnp.where(mine, aff_ref[...], 0.0), axis=0, keepdims=True)
        hit = slot_row == slot_id
        idx = jnp.sum(jnp.where(hit, token, 0.0), axis=-1, keepdims=True)
        gate = jnp.sum(jnp.where(hit, aff_row, 0.0), axis=-1, keepdims=True)
        out_ref[ei] = jnp.where(col == 0, idx, gate)
        return carry

    lax.fori_loop(0, e, per_expert, 0)


ROUTE_COLS = 8


def _route(aff, cap):
    b, e, s = aff.shape
    return pl.pallas_call(
        functools.partial(_route_body, cap),
        grid=(b,),
        in_specs=[pl.BlockSpec((None, e, s), lambda bi: (bi, 0, 0))],
        out_specs=pl.BlockSpec((None, e, cap, ROUTE_COLS), lambda bi: (bi, 0, 0, 0)),
        out_shape=jax.ShapeDtypeStruct((b, e, cap, ROUTE_COLS), F32),
        scratch_shapes=[pltpu.VMEM((e, s), F32)],
        compiler_params=_params("parallel"),
        name="ec_route",
    )(aff)


ROW_UNROLL = 8
COMBINE_EXPERTS = 4


def _gather_body(idx_ref, h_ref, o_ref):
    n = o_ref.shape[0]

    def rows(j, carry):
        base = j * ROW_UNROLL
        for u in range(ROW_UNROLL):
            o_ref[base + u] = h_ref[idx_ref[0, base + u]]
        return carry

    lax.fori_loop(0, n // ROW_UNROLL, rows, 0)


def _gather_rows(idx, hp):
    b, s, _, w = hp.shape
    n = idx.shape[2]
    return pl.pallas_call(
        _gather_body,
        grid=(b,),
        in_specs=[
            pl.BlockSpec((None, 1, n), lambda bi: (bi, 0, 0), memory_space=pltpu.SMEM),
            pl.BlockSpec((None, s, 1, w), lambda bi: (bi, 0, 0, 0)),
        ],
        out_specs=pl.BlockSpec((None, n, 1, w), lambda bi: (bi, 0, 0, 0)),
        out_shape=jax.ShapeDtypeStruct((b, n, 1, w), hp.dtype),
        compiler_params=_params("parallel"),
        name="ec_gather",
    )(idx, hp)


def _expert_body(xp_ref, gate_ref, wg_ref, wu_ref, wd_ref, y_ref, xp2_ref):
    xp2_ref[...] = xp_ref[...].reshape(xp2_ref.shape)
    xp = xp2_ref[...]
    x_hi = lax.bitcast_convert_type(xp & jnp.uint32(0xFFFF0000), F32).astype(BF16)
    x_lo = lax.bitcast_convert_type(xp << 16, F32).astype(BF16)
    xe = jnp.concatenate([x_hi, x_lo], axis=1)
    hg = jnp.dot(xe, wg_ref[...], preferred_element_type=F32)
    hu = jnp.dot(xe, wu_ref[...], preferred_element_type=F32)
    hid = (hg * jax.nn.sigmoid(hg) * hu).astype(BF16)
    y = jnp.dot(hid, wd_ref[...], preferred_element_type=F32)
    _store_rows(y_ref, y * gate_ref[:, 1:2])


def _experts(xg, route, wg, wu, wd):
    b, n, _, half = xg.shape
    e, d, f = wg.shape
    cap = n // e
    return pl.pallas_call(
        _expert_body,
        grid=(e, b),
        in_specs=[
            pl.BlockSpec((None, cap, 1, half), lambda ei, bi: (bi, ei, 0, 0)),
            pl.BlockSpec((None, None, cap, ROUTE_COLS), lambda ei, bi: (bi, ei, 0, 0)),
            pl.BlockSpec((None, d, f), lambda ei, bi: (ei, 0, 0)),
            pl.BlockSpec((None, d, f), lambda ei, bi: (ei, 0, 0)),
            pl.BlockSpec((None, f, d), lambda ei, bi: (ei, 0, 0)),
        ],
        out_specs=pl.BlockSpec((None, cap, 1, d), lambda ei, bi: (bi, ei, 0, 0)),
        out_shape=jax.ShapeDtypeStruct((b, n, 1, d), F32),
        scratch_shapes=[pltpu.VMEM((cap, half), jnp.uint32)],
        compiler_params=_params("parallel", "parallel"),
        name="ec_experts",
    )(xg, route, wg, wu, wd)


def _combine_body(idx_ref, x_ref, y_ref, o_ref):
    @pl.when(pl.program_id(1) == 0)
    def _():
        o_ref[...] = x_ref[...]

    def rows(j, carry):
        base = j * ROW_UNROLL
        sums = [o_ref[idx_ref[0, base + u]] + y_ref[base + u] for u in range(ROW_UNROLL)]
        for u in range(ROW_UNROLL):
            o_ref[idx_ref[0, base + u]] = sums[u]
        return carry

    lax.fori_loop(0, y_ref.shape[0] // ROW_UNROLL, rows, 0)


def _combine(idx, x, y, cap):
    b, s, _, d = x.shape
    rows = COMBINE_EXPERTS * cap
    assert cap % ROW_UNROLL == 0 and y.shape[1] % rows == 0
    return pl.pallas_call(
        _combine_body,
        grid=(b, y.shape[1] // rows),
        in_specs=[
            pl.BlockSpec((None, 1, rows), lambda bi, ei: (bi, 0, ei), memory_space=pltpu.SMEM),
            pl.BlockSpec((None, s, 1, d), lambda bi, ei: (bi, 0, 0, 0)),
            pl.BlockSpec((None, rows, 1, d), lambda bi, ei: (bi, ei, 0, 0)),
        ],
        out_specs=pl.BlockSpec((None, s, 1, d), lambda bi, ei: (bi, 0, 0, 0)),
        out_shape=jax.ShapeDtypeStruct(x.shape, F32),
        compiler_params=_params("parallel", "arbitrary"),
        name="ec_combine",
    )(idx, x, y)


def _final_norm_body(x_ref, g_ref, o_ref):
    o_ref[...] = x_ref[...].reshape(o_ref.shape)
    x = o_ref[...]
    ms = jnp.mean(x * x, axis=-1, keepdims=True)
    o_ref[...] = x * lax.rsqrt(ms + RMS_EPS) * g_ref[...]


def _final_norm(x, g):
    t, _, d = x.shape
    tm = TOKEN_BLOCK
    return pl.pallas_call(
        _final_norm_body,
        grid=(t // tm,),
        in_specs=[pl.BlockSpec((tm, 1, d), lambda i: (i, 0, 0)), pl.BlockSpec((1, d), lambda i: (0, 0))],
        out_specs=pl.BlockSpec((tm, d), lambda i: (i, 0)),
        out_shape=jax.ShapeDtypeStruct((t, d), F32),
        compiler_params=_params("parallel"),
        name="final_norm",
    )(x, g)


def kernel(x, attn_norm_g, ffn_norm_g, final_norm_g, a_w_in, a_w_o, a_lambda_q1, a_lambda_k1, a_lambda_q2,
           a_lambda_k2, a_subln_g, b_w_in, b_w_o, b_q_norm_g, b_k_norm_g, c_w_in, c_w_o, c_sinks,
           moe_w_router, moe_w_gate, moe_w_up, moe_w_down):
    batch, seq, d = x.shape
    depth = attn_norm_g.shape[0]
    t = batch * seq
    cap = EC_CAPACITY_FACTOR * seq // N_EXPERTS
    gqa_q = GQA_HEADS * HEAD_DIM
    gqa_kv = GQA_KV_HEADS * HEAD_DIM

    tables_1d = _lane_tables_1d(seq)
    tables_ax = _lane_tables_axial(seq)
    ones = jnp.ones((1, LANES), F32)

    xt = x.reshape(t, d)
    for i in range(depth):
        m, j = i % N_MIXERS, i // N_MIXERS
        g_attn = attn_norm_g[i].reshape(1, d)
        if m == 0:
            w = a_w_in[j].astype(BF16)
            q, k, v = _proj_in(0, xt, g_attn, w[:, :d], w[:, d:2 * d], w[:, 2 * d:], *tables_1d, ones, ones, seq)
            lambda_init = 0.8 - 0.6 * math.exp(-0.3 * i)
            vec = lambda a: a[j].reshape(1, HEAD_DIM)
            o = _diff_attn(q.reshape(batch, seq, d), k.reshape(batch, seq, d), v.reshape(batch, seq, d),
                           vec(a_lambda_q1), vec(a_lambda_k1), vec(a_lambda_q2), vec(a_lambda_k2),
                           a_subln_g[j].reshape(1, 2 * HEAD_DIM), lambda_init)
            w_o = a_w_o[j]
        else:
            w_in = b_w_in[j] if m == 1 else c_w_in[j]
            w = w_in.astype(BF16)
            if m == 1:
                tables = tables_ax
                qg = jnp.tile(b_q_norm_g[j], 2).reshape(1, LANES)
                kg = jnp.tile(b_k_norm_g[j], 2).reshape(1, LANES)
                sinks = jnp.zeros((GQA_HEADS,), F32)
            else:
                tables, qg, kg, sinks = tables_1d, ones, ones, c_sinks[j]
            q, k, v = _proj_in(m, xt, g_attn, w[:, :gqa_q], w[:, gqa_q:gqa_q + gqa_kv], w[:, gqa_q + gqa_kv:],
                               *tables, qg, kg, seq)
            o = _gqa_attn(q.reshape(batch, seq, gqa_q), k.reshape(batch, seq, gqa_kv),
                          v.reshape(batch, seq, gqa_kv), sinks, window=(m == 2))
            w_o = b_w_o[j] if m == 1 else c_w_o[j]

        xn, hp, aff = _proj_out(o.reshape(t, -1), w_o.astype(BF16), xt, ffn_norm_g[i].reshape(1, d),
                                moe_w_router[i].T, seq)
        route = _route(aff, cap)
        idx = route[..., 0].astype(I32).reshape(batch, 1, N_EXPERTS * cap)
        xg = _gather_rows(idx, hp.reshape(batch, seq, 1, d // 2))
        y = _experts(xg, route, moe_w_gate[i].astype(BF16),
                     moe_w_up[i].astype(BF16), moe_w_down[i].astype(BF16))
        xo = _combine(idx, xn.reshape(batch, seq, 1, d), y, cap)
        xt = xo.reshape(t, 1, d)
    return _final_norm(xt, final_norm_g.reshape(1, d)).reshape(batch, seq, d)
```

```python
import functools
import math

import jax
import jax.numpy as jnp
from jax import lax
from jax.experimental import pallas as pl
from jax.experimental.pallas import tpu as pltpu

F32 = jnp.float32
BF16 = jnp.bfloat16
I32 = jnp.int32

D_MODEL = 1024
HEAD_DIM = 64
GQA_HEADS = 16
GQA_KV_HEADS = 4
GRID_W = 64
WINDOW = 128
ROPE_THETA = 10000.0
N_EXPERTS = 16
EC_CAPACITY_FACTOR = 2
RMS_EPS = 1e-6
N_MIXERS = 3

LANES = 128
VMEM_LIMIT = 56 * 1024 * 1024

TOKEN_BLOCK = 512
Q_TILE = 256
NT_DIMS = (((1,), (1,)), ((), ()))


def _params(*sem):
    return pltpu.CompilerParams(dimension_semantics=sem, vmem_limit_bytes=VMEM_LIMIT)


def _rope_tables(pos, dim):
    inv = ROPE_THETA ** (-jnp.arange(0, dim, 2, dtype=F32) / dim)
    ang = pos.astype(F32)[:, None] * inv[None, :]
    return jnp.cos(ang), jnp.sin(ang)


def _lane_tables_1d(seq):
    cos, sin = _rope_tables(jnp.arange(seq), HEAD_DIM)
    cos_l = jnp.tile(cos, (1, LANES // 32))
    sign = jnp.where((jnp.arange(LANES) & 32) == 0, -1.0, 1.0).astype(F32)
    sin_l = jnp.tile(sin, (1, LANES // 32)) * sign[None, :]
    return cos_l, sin_l


def _lane_tables_axial(seq):
    rows = seq // GRID_W
    row_idx = jnp.repeat(jnp.arange(rows), GRID_W)
    col_idx = jnp.tile(jnp.arange(GRID_W), rows)
    rc, rs = _rope_tables(row_idx, HEAD_DIM // 2)
    cc, cs = _rope_tables(col_idx, HEAD_DIM // 2)
    cos64 = jnp.concatenate([rc, rc, cc, cc], axis=1)
    sin64 = jnp.concatenate([-rs, rs, -cs, cs], axis=1)
    return jnp.tile(cos64, (1, 2)), jnp.tile(sin64, (1, 2))


def _rope_chunk(c, cos, sin, half):
    lane = lax.broadcasted_iota(I32, c.shape, 1)
    low = (lane & half) == 0
    partner = jnp.where(low, pltpu.roll(c, LANES - half, 1), pltpu.roll(c, half, 1))
    return c * cos + partner * sin


def _head_rms_chunk(c, gain):
    lane = lax.broadcasted_iota(I32, c.shape, 1)
    low = lane < HEAD_DIM
    cc = c * c
    s_lo = jnp.sum(jnp.where(low, cc, 0.0), axis=-1, keepdims=True)
    s_hi = jnp.sum(jnp.where(low, 0.0, cc), axis=-1, keepdims=True)
    ms = jnp.where(low, s_lo, s_hi) * (1.0 / HEAD_DIM)
    return c * lax.rsqrt(ms + RMS_EPS) * gain


def _load_residual(x_ref, x2_ref):
    if len(x_ref.shape) == 2:
        return x_ref[...]
    x2_ref[...] = x_ref[...].reshape(x2_ref.shape)
    return x2_ref[...]


def _residual_spec(x, tm):
    if x.ndim == 2:
        return pl.BlockSpec((tm, x.shape[-1]), lambda i: (i, 0))
    return pl.BlockSpec((tm, 1, x.shape[-1]), lambda i: (i, 0, 0))


def _proj_in_body(mixer, x_ref, g_ref, wq_ref, wk_ref, wv_ref, cos_ref, sin_ref, qg_ref, kg_ref,
                  q_ref, k_ref, v_ref, x2_ref):
    x = _load_residual(x_ref, x2_ref)
    ms = jnp.mean(x * x, axis=-1, keepdims=True)
    h = (x * lax.rsqrt(ms + RMS_EPS) * g_ref[...]).astype(BF16)
    cos = cos_ref[...]
    sin = sin_ref[...]
    half = 16 if mixer == 1 else 32

    def finish(acc, out_ref, scale, gain_ref):
        for j in range(acc.shape[1] // LANES):
            c = acc[:, j * LANES:(j + 1) * LANES]
            if mixer == 1:
                c = _head_rms_chunk(c, gain_ref[...])
            c = _rope_chunk(c, cos, sin, half)
            if scale != 1.0:
                c = c * scale
            out_ref[:, j * LANES:(j + 1) * LANES] = c.astype(BF16)

    project = lambda w_ref: jnp.dot(h, w_ref[...].astype(BF16), preferred_element_type=F32)
    finish(project(wq_ref), q_ref, Q_SCALE, qg_ref)
    finish(project(wk_ref), k_ref, 1.0, kg_ref)
    v_ref[...] = project(wv_ref).astype(BF16)


def _proj_in(mixer, x, g, w, nq, nk, cos_l, sin_l, qg, kg, seq):
    t, d = x.shape[0], x.shape[-1]
    tm = TOKEN_BLOCK
    nv = nk
    assert w.shape == (d, nq + nk + nv) and nq % nk == 0
    sblocks = seq // tm
    row = lambda i: (i, 0)
    fixed = lambda i: (0, 0)
    tab = lambda i: (i % sblocks, 0)
    return pl.pallas_call(
        functools.partial(_proj_in_body, mixer),
        grid=(t // tm,),
        in_specs=[
            _residual_spec(x, tm),
            pl.BlockSpec((1, d), fixed),
            pl.BlockSpec((d, nq), fixed),
            pl.BlockSpec((d, nk), lambda i: (0, nq // nk)),
            pl.BlockSpec((d, nv), lambda i: (0, nq // nk + 1)),
            pl.BlockSpec((tm, LANES), tab),
            pl.BlockSpec((tm, LANES), tab),
            pl.BlockSpec((1, LANES), fixed),
            pl.BlockSpec((1, LANES), fixed),
        ],
        out_specs=[
            pl.BlockSpec((tm, nq), row),
            pl.BlockSpec((tm, nk), row),
            pl.BlockSpec((tm, nv), row),
        ],
        out_shape=[
            jax.ShapeDtypeStruct((t, nq), BF16),
            jax.ShapeDtypeStruct((t, nk), BF16),
            jax.ShapeDtypeStruct((t, nv), BF16),
        ],
        scratch_shapes=[pltpu.VMEM((tm, d), F32)],
        compiler_params=_params("parallel"),
        name=f"proj_in_m{mixer}",
    )(x, g, w, w, w, cos_l, sin_l, qg, kg)


LOG2E = 1.4426950408889634
Q_SCALE = HEAD_DIM ** -0.5 * LOG2E
KEY_CHUNK = 512


def _key_chunk(span):
    assert span % KEY_CHUNK == 0
    return KEY_CHUNK


def _attention_chains(n_chains, q_of, k_of, v_aug_of, finish, s_ref, p_ref, span, mask_of=None, floor_of=None):
    kc = _key_chunk(span)
    pieces = span // kc

    def scores(c, j, q):
        s = lax.dot_general(q, k_of(c, j), NT_DIMS, preferred_element_type=F32)
        if mask_of is not None:
            s = jnp.where(mask_of(j), s, -jnp.inf)
        s_ref[c % 2, :, j * kc:(j + 1) * kc] = s

    q = q_of(0)
    for j in range(pieces):
        scores(0, j, q)
    for c in range(n_chains):
        slot = c % 2
        m = jnp.max(s_ref[slot], axis=-1, keepdims=True)
        if floor_of is not None:
            m = jnp.maximum(m, floor_of(c))
        q = q_of(c + 1) if c + 1 < n_chains else None
        for j in range(pieces):
            if q is not None:
                scores(c + 1, j, q)
            e = jnp.exp2(s_ref[slot, :, j * kc:(j + 1) * kc] - m)
            p_ref[slot, :, j * kc:(j + 1) * kc] = e.astype(BF16)
        finish(c, jnp.dot(p_ref[slot], v_aug_of(c), preferred_element_type=F32), m)


def _fill_v_aug(vaug_ref, v_ref):
    vaug_ref[:, :LANES] = v_ref[...]
    vaug_ref[:, LANES:] = jnp.ones((vaug_ref.shape[0], LANES), BF16)


def _lane_tiles(e):
    return [e[:, j * LANES:(j + 1) * LANES] for j in range(e.shape[1] // LANES)]


def _tree(op, parts):
    while len(parts) > 1:
        parts = [op(a, b) for a, b in zip(parts[::2], parts[1::2])]
    return parts[0]


def _lane_tile_sum(e):
    return _tree(jnp.add, _lane_tiles(e))


def _lane_tile_max(pieces):
    return _tree(jnp.maximum, [_tree(jnp.maximum, _lane_tiles(p)) for p in pieces])


def _diff_attn_body(lambda_init, q_ref, k_ref, v_ref, lq1_ref, lk1_ref, lq2_ref, lk2_ref, sg_ref, o_ref,
                    s00, s01, s10, s11, p0, p1, linv0, linv1, acc_ref):
    s_ref, p_ref, linv_ref = ((s00, s01), (s10, s11)), (p0, p1), (linv0, linv1)
    tq = s00.shape[0]
    span = k_ref.shape[0]
    kc = _key_chunk(span)
    pieces = span // kc
    tiles = q_ref.shape[0] // tq
    lane = lax.broadcasted_iota(I32, (tq, LANES), 1)
    lam = (jnp.exp(jnp.sum(lq1_ref[...] * lk1_ref[...], axis=-1, keepdims=True))
           - jnp.exp(jnp.sum(lq2_ref[...] * lk2_ref[...], axis=-1, keepdims=True))
           + lambda_init)

    def q_of(t, mp):
        q = q_ref[t * tq:(t + 1) * tq, :]
        keep = lane < HEAD_DIM if mp == 0 else lane >= HEAD_DIM
        return jnp.where(keep, q, jnp.zeros_like(q))

    def scores(t, mp, j, q):
        s_ref[t % 2][mp][:, j * kc:(j + 1) * kc] = lax.dot_general(
            q, k_ref[j * kc:(j + 1) * kc, :], NT_DIMS, preferred_element_type=F32)

    qs = [q_of(0, 0), q_of(0, 1)]
    for j in range(pieces):
        scores(0, 0, j, qs[0])
        scores(0, 1, j, qs[1])
    blocks = tq // SOFTMAX_ROWS
    cols = [slice(j * kc, (j + 1) * kc) for j in range(pieces)]

    def softmax_block(t, r):
        slot = t % 2
        rows = slice(r * SOFTMAX_ROWS, (r + 1) * SOFTMAX_ROWS)
        sums = []
        for mp in range(2):
            tile_max = _lane_tile_max([s_ref[slot][mp][rows, c] for c in cols])
            m = jnp.max(tile_max, axis=-1, keepdims=True)
            part = jnp.zeros((SOFTMAX_ROWS, LANES), F32)
            for c in cols:
                e = jnp.exp2(s_ref[slot][mp][rows, c] - m)
                s_ref[slot][mp][rows, c] = e
                part = part + _lane_tile_sum(e)
            sums.append(jnp.sum(part, axis=-1, keepdims=True))
        ratio = lam * sums[0] / sums[1]
        for c in cols:
            a = s_ref[slot][0][rows, c] - s_ref[slot][1][rows, c] * ratio
            p_ref[slot][rows, c] = a.astype(BF16)
        linv_ref[slot][rows, :] = jnp.broadcast_to(1.0 / sums[0], (SOFTMAX_ROWS, LANES))

    def finish(t):
        o = acc_ref[...] * linv_ref[t % 2][...]
        ms = jnp.mean(o * o, axis=-1, keepdims=True)
        o = o * lax.rsqrt(ms + RMS_EPS) * sg_ref[...] * (1.0 - lambda_init)
        o_ref[t * tq:(t + 1) * tq, :] = o.astype(BF16)

    for t in range(tiles + 1):
        ahead = []
        if t + 1 < tiles:
            qs = [q_of(t + 1, 0), q_of(t + 1, 1)]
            ahead = [(mp, j) for mp in range(2) for j in range(pieces)]
        per_block = -(-len(ahead) // blocks)
        for r in range(max(blocks, pieces)):
            for mp, j in ahead[r * per_block:(r + 1) * per_block]:
                scores(t + 1, mp, j, qs[mp])
            if t >= 1 and r < pieces:
                part_o = jnp.dot(p_ref[(t - 1) % 2][:, cols[r]], v_ref[cols[r], :], preferred_element_type=F32)
                acc_ref[...] = part_o if r == 0 else acc_ref[...] + part_o
            if t < tiles and r < blocks:
                softmax_block(t, r)
        if t >= 1:
            finish(t - 1)


DIFF_Q_TILE = 256
DIFF_Q_TILES = 4
SOFTMAX_ROWS = 32


def _attn_scratch(tq, span, keys):
    return [pltpu.VMEM((keys, 2 * LANES), BF16), pltpu.VMEM((2, tq, span), F32), pltpu.VMEM((2, tq, span), BF16)]


def _diff_attn(q, k, v, lq1, lk1, lq2, lk2, subln_g, lambda_init):
    b, s, n = q.shape
    heads = n // LANES
    tq = DIFF_Q_TILE
    rows = DIFF_Q_TILES * tq
    vec = pl.BlockSpec((1, HEAD_DIM), lambda bi, h, i: (0, 0))
    return pl.pallas_call(
        functools.partial(_diff_attn_body, lambda_init),
        grid=(b, heads, s // rows),
        in_specs=[
            pl.BlockSpec((None, rows, LANES), lambda bi, h, i: (bi, i, h)),
            pl.BlockSpec((None, s, LANES), lambda bi, h, i: (bi, 0, h)),
            pl.BlockSpec((None, s, LANES), lambda bi, h, i: (bi, 0, h)),
            vec, vec, vec, vec,
            pl.BlockSpec((1, LANES), lambda bi, h, i: (0, 0)),
        ],
        out_specs=pl.BlockSpec((None, rows, LANES), lambda bi, h, i: (bi, i, h)),
        out_shape=jax.ShapeDtypeStruct((b, s, n), BF16),
        scratch_shapes=([pltpu.VMEM((tq, s), F32)] * 4 + [pltpu.VMEM((tq, s), BF16)] * 2
                        + [pltpu.VMEM((tq, LANES), F32)] * 3),
        compiler_params=_params("parallel", "parallel", "parallel"),
        name="diff_attn",
    )(q, k, v, lq1, lk1, lq2, lk2, subln_g)


def _place_head(x, src_half, dst_half):
    return x if src_half == dst_half else pltpu.roll(x, HEAD_DIM, 1)


def _gqa_pair_body(window, q_ref, k_ref, v_ref, sink_ref, o_ref, vaug_ref, s_ref, p_ref):
    pl.when(pl.program_id(2) == 0)(functools.partial(_fill_v_aug, vaug_ref, v_ref))
    tq = q_ref.shape[0]
    span = s_ref.shape[2]
    kc = _key_chunk(span)
    group = GQA_HEADS // GQA_KV_HEADS
    pair = pl.program_id(1)
    lane = lax.broadcasted_iota(I32, (tq, LANES), 1)
    if window:
        i = pl.program_id(2)
        start = pl.multiple_of(jnp.clip(i * tq - WINDOW, 0, k_ref.shape[0] - span), LANES)
        qpos = i * tq + lax.broadcasted_iota(I32, (tq, kc), 0)
        kpos = start + lax.broadcasted_iota(I32, (tq, kc), 1)
        mask_of = lambda j: jnp.abs(qpos - (kpos + j * kc)) <= WINDOW
        k_of = lambda c, j: k_ref[pl.ds(start + j * kc, kc), :]
        v_aug_of = lambda c: vaug_ref[pl.ds(start, span), :]
        sink2 = lambda c: sink_ref[pair * 2 * group + c] * LOG2E
    else:
        mask_of = sink2 = None
        k_of = lambda c, j: k_ref[j * kc:(j + 1) * kc, :]
        v_aug_of = lambda c: vaug_ref[...]

    def q_of(c):
        hh, kvl = c % 2, c // group
        qc = q_ref[:, (c // 2) * LANES:(c // 2 + 1) * LANES]
        qh = jnp.where((lane >= hh * HEAD_DIM) & (lane < (hh + 1) * HEAD_DIM), qc, jnp.zeros_like(qc))
        return qh if hh == kvl else _place_head(qh.astype(F32), hh, kvl).astype(BF16)

    first = []

    def finish(c, o_aug, m):
        hh, kvl = c % 2, c // group
        total = o_aug[:, LANES:LANES + 1]
        if window:
            total = total + jnp.exp2(sink2(c) - m)
        o = _place_head(o_aug[:, :LANES] * (1.0 / total), kvl, hh)
        if hh == 0:
            first.append(o)
        else:
            o_ref[:, (c // 2) * LANES:(c // 2 + 1) * LANES] = jnp.where(lane < HEAD_DIM, first.pop(), o).astype(BF16)

    _attention_chains(2 * group, q_of, k_of, v_aug_of, finish, s_ref, p_ref, span, mask_of, sink2)


def _gqa_attn(q, k, v, sinks, window):
    b, s, n = q.shape
    pairs = GQA_KV_HEADS // 2
    qw = n // pairs
    tq = Q_TILE
    span = tq + 2 * WINDOW if window else s
    return pl.pallas_call(
        functools.partial(_gqa_pair_body, window),
        grid=(b, pairs, s // tq),
        in_specs=[
            pl.BlockSpec((None, tq, qw), lambda bi, p, i: (bi, i, p)),
            pl.BlockSpec((None, s, LANES), lambda bi, p, i: (bi, 0, p)),
            pl.BlockSpec((None, s, LANES), lambda bi, p, i: (bi, 0, p)),
            pl.BlockSpec(memory_space=pltpu.SMEM),
        ],
        out_specs=pl.BlockSpec((None, tq, qw), lambda bi, p, i: (bi, i, p)),
        out_shape=jax.ShapeDtypeStruct((b, s, n), BF16),
        scratch_shapes=_attn_scratch(tq, span, s),
        compiler_params=_params("parallel", "parallel", "arbitrary"),
        name="window_attn" if window else "axial_attn",
    )(q, k, v, sinks)


def _store_rows(ref, val):
    ref[...] = val.reshape(ref.shape)


def _proj_out_body(o_ref, wo_ref, x_ref, g_ref, wr_ref, xn_ref, hp_ref, aff_ref, x2_ref):
    xn = _load_residual(x_ref, x2_ref) + jnp.dot(o_ref[...], wo_ref[...].astype(BF16),
                                                  preferred_element_type=F32)
    _store_rows(xn_ref, xn)
    ms = jnp.mean(xn * xn, axis=-1, keepdims=True)
    h = xn * lax.rsqrt(ms + RMS_EPS) * g_ref[...]
    hb = h.astype(BF16)
    hb32 = hb.astype(F32)
    half = h.shape[1] // 2
    hi = lax.bitcast_convert_type(hb32[:, :half], jnp.uint32)
    lo = lax.bitcast_convert_type(hb32[:, half:], jnp.uint32)
    _store_rows(hp_ref, hi | (lo >> 16))
    wr = wr_ref[...]
    wr_hi = wr.astype(BF16)
    wr_lo = (wr - wr_hi.astype(F32)).astype(BF16)
    h_lo = (h - hb32).astype(BF16)
    logits = (lax.dot_general(wr_hi, hb, NT_DIMS, preferred_element_type=F32)
              + lax.dot_general(wr_lo, hb, NT_DIMS, preferred_element_type=F32)
              + lax.dot_general(wr_hi, h_lo, NT_DIMS, preferred_element_type=F32))
    m = jnp.max(logits, axis=0, keepdims=True)
    e = jnp.exp(logits - m)
    aff_ref[...] = e / jnp.sum(e, axis=0, keepdims=True)


def _proj_out(o, wo, x, g, wr_t, seq):
    t, d = x.shape[0], x.shape[-1]
    tm = TOKEN_BLOCK
    e = wr_t.shape[0]
    sblocks = seq // tm
    row = lambda i: (i, 0)
    fixed = lambda i: (0, 0)
    return pl.pallas_call(
        _proj_out_body,
        grid=(t // tm,),
        in_specs=[
            pl.BlockSpec((tm, o.shape[1]), row),
            pl.BlockSpec(wo.shape, fixed),
            _residual_spec(x, tm),
            pl.BlockSpec((1, d), fixed),
            pl.BlockSpec((e, d), fixed),
        ],
        out_specs=[
            pl.BlockSpec((tm, 1, d), lambda i: (i, 0, 0)),
            pl.BlockSpec((tm, 1, d // 2), lambda i: (i, 0, 0)),
            pl.BlockSpec((None, e, tm), lambda i: (i // sblocks, 0, i % sblocks)),
        ],
        out_shape=[
            jax.ShapeDtypeStruct((t, 1, d), F32),
            jax.ShapeDtypeStruct((t, 1, d // 2), jnp.uint32),
            jax.ShapeDtypeStruct((t // seq, e, seq), F32),
        ],
        scratch_shapes=[pltpu.VMEM((tm, d), F32)],
        compiler_params=_params("parallel"),
        name="proj_out_router",
    )(o, wo, x, g, wr_t)


PREFIX_CHUNK = 256


def _exclusive_prefix(mask_f):
    e, s = mask_f.shape
    r = lax.broadcasted_iota(I32, (PREFIX_CHUNK, PREFIX_CHUNK), 0)
    c = lax.broadcasted_iota(I32, (PREFIX_CHUNK, PREFIX_CHUNK), 1)
    upper = jnp.where(r < c, 1.0, 0.0).astype(BF16)
    offset = jnp.zeros((e, 1), F32)
    outs = []
    for j in range(s // PREFIX_CHUNK):
        mj = mask_f[:, j * PREFIX_CHUNK:(j + 1) * PREFIX_CHUNK]
        outs.append(jnp.dot(mj.astype(BF16), upper, preferred_element_type=F32) + offset)
        offset = offset + jnp.sum(mj, axis=-1, keepdims=True)
    return jnp.concatenate(outs, axis=1)


def _route_body(cap, aff_ref, out_ref, slot_ref):
    aff = aff_ref[...]
    e, s = aff.shape
    bits = lax.bitcast_convert_type(aff, I32)

    def count(mask):
        return jnp.sum(jnp.where(mask, 1.0, 0.0), axis=-1, keepdims=True)

    def search(i, thr):
        cand = thr | lax.shift_left(jnp.int32(1), 30 - i)
        return jnp.where(count(bits >= cand) >= cap, cand, thr)

    thr = lax.fori_loop(0, 31, search, jnp.zeros((e, 1), I32))
    gt = bits > thr
    eq = bits == thr
    need = cap - count(gt)
    eq_rank = _exclusive_prefix(jnp.where(eq, 1.0, 0.0))
    sel = gt | (eq & (eq_rank < need))
    sel_f = jnp.where(sel, 1.0, 0.0)
    slot_ref[...] = jnp.where(sel, _exclusive_prefix(sel_f), -1.0)

    slot_id = lax.broadcasted_iota(I32, (cap, s), 0).astype(F32)
    token = lax.broadcasted_iota(I32, (1, s), 1).astype(F32)
    col = lax.broadcasted_iota(I32, (cap, out_ref.shape[2]), 1)
    expert = lax.broadcasted_iota(I32, (e, s), 0)

    def per_expert(ei, carry):
        mine = expert == ei
        slot_row = jnp.sum(jnp.where(mine, slot_ref[...], 0.0), axis=0, keepdims=True)
        aff_row = jnp.sum(jnp.where(mine, aff_ref[...], 0.0), axis=0, keepdims=True)
        hit = slot_row == slot_id
        idx = jnp.sum(jnp.where(hit, token, 0.0), axis=-1, keepdims=True)
        gate = jnp.sum(jnp.where(hit, aff_row, 0.0), axis=-1, keepdims=True)
        out_ref[ei] = jnp.where(col == 0, idx, gate)
        return carry

    lax.fori_loop(0, e, per_expert, 0)


ROUTE_COLS = 8


def _route(aff, cap):
    b, e, s = aff.shape
    return pl.pallas_call(
        functools.partial(_route_body, cap),
        grid=(b,),
        in_specs=[pl.BlockSpec((None, e, s), lambda bi: (bi, 0, 0))],
        out_specs=pl.BlockSpec((None, e, cap, ROUTE_COLS), lambda bi: (bi, 0, 0, 0)),
        out_shape=jax.ShapeDtypeStruct((b, e, cap, ROUTE_COLS), F32),
        scratch_shapes=[pltpu.VMEM((e, s), F32)],
        compiler_params=_params("parallel"),
        name="ec_route",
    )(aff)


ROW_UNROLL = 8
COMBINE_EXPERTS = 4


def _gather_body(idx_ref, h_ref, o_ref):
    n = o_ref.shape[0]

    def rows(j, carry):
        base = j * ROW_UNROLL
        for u in range(ROW_UNROLL):
            o_ref[base + u] = h_ref[idx_ref[0, base + u]]
        return carry

    lax.fori_loop(0, n // ROW_UNROLL, rows, 0)


def _gather_rows(idx, hp):
    b, s, _, w = hp.shape
    n = idx.shape[2]
    return pl.pallas_call(
        _gather_body,
        grid=(b,),
        in_specs=[
            pl.BlockSpec((None, 1, n), lambda bi: (bi, 0, 0), memory_space=pltpu.SMEM),
            pl.BlockSpec((None, s, 1, w), lambda bi: (bi, 0, 0, 0)),
        ],
        out_specs=pl.BlockSpec((None, n, 1, w), lambda bi: (bi, 0, 0, 0)),
        out_shape=jax.ShapeDtypeStruct((b, n, 1, w), hp.dtype),
        compiler_params=_params("parallel"),
        name="ec_gather",
    )(idx, hp)


def _expert_body(xp_ref, gate_ref, wg_ref, wu_ref, wd_ref, y_ref, xp2_ref):
    rows, cap = xp_ref.shape[0], xp_ref.shape[1]
    for r in range(rows):
        xp2_ref[r * cap:(r + 1) * cap, :] = xp_ref[r].reshape(cap, xp2_ref.shape[1])
    xp = xp2_ref[...]
    x_hi = lax.bitcast_convert_type(xp & jnp.uint32(0xFFFF0000), F32).astype(BF16)
    x_lo = lax.bitcast_convert_type(xp << 16, F32).astype(BF16)
    xe = jnp.concatenate([x_hi, x_lo], axis=1)
    hg = jnp.dot(xe, wg_ref[...], preferred_element_type=F32)
    hu = jnp.dot(xe, wu_ref[...], preferred_element_type=F32)
    hid = (hg * jax.nn.sigmoid(hg) * hu).astype(BF16)
    y = jnp.dot(hid, wd_ref[...], preferred_element_type=F32)
    for r in range(rows):
        _store_rows(y_ref.at[r], y[r * cap:(r + 1) * cap, :] * gate_ref[r][:, 1:2])


EXPERT_BATCH_ROWS = 2


def _experts(xg, route, wg, wu, wd):
    b, n, _, half = xg.shape
    e, d, f = wg.shape
    cap = n // e
    rows = EXPERT_BATCH_ROWS
    return pl.pallas_call(
        _expert_body,
        grid=(e, b // rows),
        in_specs=[
            pl.BlockSpec((rows, cap, 1, half), lambda ei, bi: (bi, ei, 0, 0)),
            pl.BlockSpec((rows, None, cap, ROUTE_COLS), lambda ei, bi: (bi, ei, 0, 0)),
            pl.BlockSpec((None, d, f), lambda ei, bi: (ei, 0, 0)),
            pl.BlockSpec((None, d, f), lambda ei, bi: (ei, 0, 0)),
            pl.BlockSpec((None, f, d), lambda ei, bi: (ei, 0, 0)),
        ],
        out_specs=pl.BlockSpec((rows, cap, 1, d), lambda ei, bi: (bi, ei, 0, 0)),
        out_shape=jax.ShapeDtypeStruct((b, n, 1, d), F32),
        scratch_shapes=[pltpu.VMEM((rows * cap, half), jnp.uint32)],
        compiler_params=_params("parallel", "parallel"),
        name="ec_experts",
    )(xg, route, wg, wu, wd)


def _combine_body(idx_ref, x_ref, y_ref, o_ref):
    @pl.when(pl.program_id(1) == 0)
    def _():
        o_ref[...] = x_ref[...]

    def rows(j, carry):
        base = j * ROW_UNROLL
        sums = [o_ref[idx_ref[0, base + u]] + y_ref[base + u] for u in range(ROW_UNROLL)]
        for u in range(ROW_UNROLL):
            o_ref[idx_ref[0, base + u]] = sums[u]
        return carry

    lax.fori_loop(0, y_ref.shape[0] // ROW_UNROLL, rows, 0)


def _combine(idx, x, y, cap):
    b, s, _, d = x.shape
    rows = COMBINE_EXPERTS * cap
    assert cap % ROW_UNROLL == 0 and y.shape[1] % rows == 0
    return pl.pallas_call(
        _combine_body,
        grid=(b, y.shape[1] // rows),
        in_specs=[
            pl.BlockSpec((None, 1, rows), lambda bi, ei: (bi, 0, ei), memory_space=pltpu.SMEM),
            pl.BlockSpec((None, s, 1, d), lambda bi, ei: (bi, 0, 0, 0)),
            pl.BlockSpec((None, rows, 1, d), lambda bi, ei: (bi, ei, 0, 0)),
        ],
        out_specs=pl.BlockSpec((None, s, 1, d), lambda bi, ei: (bi, 0, 0, 0)),
        out_shape=jax.ShapeDtypeStruct(x.shape, F32),
        compiler_params=_params("parallel", "arbitrary"),
        name="ec_combine",
    )(idx, x, y)


def _final_norm_body(x_ref, g_ref, o_ref):
    o_ref[...] = x_ref[...].reshape(o_ref.shape)
    x = o_ref[...]
    ms = jnp.mean(x * x, axis=-1, keepdims=True)
    o_ref[...] = x * lax.rsqrt(ms + RMS_EPS) * g_ref[...]


def _final_norm(x, g):
    t, _, d = x.shape
    tm = TOKEN_BLOCK
    return pl.pallas_call(
        _final_norm_body,
        grid=(t // tm,),
        in_specs=[pl.BlockSpec((tm, 1, d), lambda i: (i, 0, 0)), pl.BlockSpec((1, d), lambda i: (0, 0))],
        out_specs=pl.BlockSpec((tm, d), lambda i: (i, 0)),
        out_shape=jax.ShapeDtypeStruct((t, d), F32),
        compiler_params=_params("parallel"),
        name="final_norm",
    )(x, g)


def kernel(x, attn_norm_g, ffn_norm_g, final_norm_g, a_w_in, a_w_o, a_lambda_q1, a_lambda_k1, a_lambda_q2,
           a_lambda_k2, a_subln_g, b_w_in, b_w_o, b_q_norm_g, b_k_norm_g, c_w_in, c_w_o, c_sinks,
           moe_w_router, moe_w_gate, moe_w_up, moe_w_down):
    batch, seq, d = x.shape
    depth = attn_norm_g.shape[0]
    t = batch * seq
    cap = EC_CAPACITY_FACTOR * seq // N_EXPERTS
    gqa_q = GQA_HEADS * HEAD_DIM
    gqa_kv = GQA_KV_HEADS * HEAD_DIM

    tables_1d = _lane_tables_1d(seq)
    tables_ax = _lane_tables_axial(seq)
    ones = jnp.ones((1, LANES), F32)

    xt = x.reshape(t, d)
    for i in range(depth):
        m, j = i % N_MIXERS, i // N_MIXERS
        g_attn = attn_norm_g[i].reshape(1, d)
        if m == 0:
            q, k, v = _proj_in(0, xt, g_attn, a_w_in[j], d, d, *tables_1d, ones, ones, seq)
            lambda_init = 0.8 - 0.6 * math.exp(-0.3 * i)
            vec = lambda a: a[j].reshape(1, HEAD_DIM)
            o = _diff_attn(q.reshape(batch, seq, d), k.reshape(batch, seq, d), v.reshape(batch, seq, d),
                           vec(a_lambda_q1), vec(a_lambda_k1), vec(a_lambda_q2), vec(a_lambda_k2),
                           a_subln_g[j].reshape(1, 2 * HEAD_DIM), lambda_init)
            w_o = a_w_o[j]
        else:
            w_in = b_w_in[j] if m == 1 else c_w_in[j]
            if m == 1:
                tables = tables_ax
                qg = jnp.tile(b_q_norm_g[j], 2).reshape(1, LANES)
                kg = jnp.tile(b_k_norm_g[j], 2).reshape(1, LANES)
                sinks = jnp.zeros((GQA_HEADS,), F32)
            else:
                tables, qg, kg, sinks = tables_1d, ones, ones, c_sinks[j]
            q, k, v = _proj_in(m, xt, g_attn, w_in, gqa_q, gqa_kv, *tables, qg, kg, seq)
            o = _gqa_attn(q.reshape(batch, seq, gqa_q), k.reshape(batch, seq, gqa_kv),
                          v.reshape(batch, seq, gqa_kv), sinks, window=(m == 2))
            w_o = b_w_o[j] if m == 1 else c_w_o[j]

        xn, hp, aff = _proj_out(o.reshape(t, -1), w_o, xt, ffn_norm_g[i].reshape(1, d),
                                moe_w_router[i].T, seq)
        route = _route(aff, cap)
        idx = route[..., 0].astype(I32).reshape(batch, 1, N_EXPERTS * cap)
        xg = _gather_rows(idx, hp.reshape(batch, seq, 1, d // 2))
        y = _experts(xg, route, moe_w_gate[i].astype(BF16),
                     moe_w_up[i].astype(BF16), moe_w_down[i].astype(BF16))
        xo = _combine(idx, xn.reshape(batch, seq, 1, d), y, cap)
        xt = xo.reshape(t, 1, d)
    return _final_norm(xt, final_norm_g.reshape(1, d)).reshape(batch, seq, d)
```

```python
import functools
import math

import jax
import jax.numpy as jnp
from jax import lax
from jax.experimental import pallas as pl
from jax.experimental.pallas import tpu as pltpu

F32 = jnp.float32
BF16 = jnp.bfloat16
I32 = jnp.int32

D_MODEL = 1024
HEAD_DIM = 64
GQA_HEADS = 16
GQA_KV_HEADS = 4
GRID_W = 64
WINDOW = 128
ROPE_THETA = 10000.0
N_EXPERTS = 16
EC_CAPACITY_FACTOR = 2
RMS_EPS = 1e-6
N_MIXERS = 3

LANES = 128
VMEM_LIMIT = 56 * 1024 * 1024

TOKEN_BLOCK = 1024
Q_TILE = 256
NT_DIMS = (((1,), (1,)), ((), ()))


def _params(*sem):
    return pltpu.CompilerParams(dimension_semantics=sem, vmem_limit_bytes=VMEM_LIMIT)


def _rope_tables(pos, dim):
    inv = ROPE_THETA ** (-jnp.arange(0, dim, 2, dtype=F32) / dim)
    ang = pos.astype(F32)[:, None] * inv[None, :]
    return jnp.cos(ang), jnp.sin(ang)


def _lane_tables_1d(seq):
    cos, sin = _rope_tables(jnp.arange(seq), HEAD_DIM)
    cos_l = jnp.tile(cos, (1, LANES // 32))
    sign = jnp.where((jnp.arange(LANES) & 32) == 0, -1.0, 1.0).astype(F32)
    sin_l = jnp.tile(sin, (1, LANES // 32)) * sign[None, :]
    return cos_l, sin_l


def _lane_tables_axial(seq):
    rows = seq // GRID_W
    row_idx = jnp.repeat(jnp.arange(rows), GRID_W)
    col_idx = jnp.tile(jnp.arange(GRID_W), rows)
    rc, rs = _rope_tables(row_idx, HEAD_DIM // 2)
    cc, cs = _rope_tables(col_idx, HEAD_DIM // 2)
    cos64 = jnp.concatenate([rc, rc, cc, cc], axis=1)
    sin64 = jnp.concatenate([-rs, rs, -cs, cs], axis=1)
    return jnp.tile(cos64, (1, 2)), jnp.tile(sin64, (1, 2))


def _rope_chunk(c, cos, sin, half):
    lane = lax.broadcasted_iota(I32, c.shape, 1)
    low = (lane & half) == 0
    partner = jnp.where(low, pltpu.roll(c, LANES - half, 1), pltpu.roll(c, half, 1))
    return c * cos + partner * sin


def _head_rms_chunk(c, gain):
    lane = lax.broadcasted_iota(I32, c.shape, 1)
    low = lane < HEAD_DIM
    cc = c * c
    s_lo = jnp.sum(jnp.where(low, cc, 0.0), axis=-1, keepdims=True)
    s_hi = jnp.sum(jnp.where(low, 0.0, cc), axis=-1, keepdims=True)
    ms = jnp.where(low, s_lo, s_hi) * (1.0 / HEAD_DIM)
    return c * lax.rsqrt(ms + RMS_EPS) * gain


def _load_residual(x_ref, x2_ref):
    if len(x_ref.shape) == 2:
        return x_ref[...]
    x2_ref[...] = x_ref[...].reshape(x2_ref.shape)
    return x2_ref[...]


def _residual_spec(x, tm):
    if x.ndim == 2:
        return pl.BlockSpec((tm, x.shape[-1]), lambda i: (i, 0))
    return pl.BlockSpec((tm, 1, x.shape[-1]), lambda i: (i, 0, 0))


def _proj_in_body(mixer, x_ref, g_ref, wq_ref, wk_ref, wv_ref, cos_ref, sin_ref, qg_ref, kg_ref,
                  q_ref, k_ref, v_ref, x2_ref):
    x = _load_residual(x_ref, x2_ref)
    ms = jnp.mean(x * x, axis=-1, keepdims=True)
    h = (x * lax.rsqrt(ms + RMS_EPS) * g_ref[...]).astype(BF16)
    cos = cos_ref[...]
    sin = sin_ref[...]
    half = 16 if mixer == 1 else 32

    def finish(acc, out_ref, scale, gain_ref):
        for j in range(acc.shape[1] // LANES):
            c = acc[:, j * LANES:(j + 1) * LANES]
            if mixer == 1:
                c = _head_rms_chunk(c, gain_ref[...])
            c = _rope_chunk(c, cos, sin, half)
            if scale != 1.0:
                c = c * scale
            out_ref[:, j * LANES:(j + 1) * LANES] = c.astype(BF16)

    project = lambda w_ref: jnp.dot(h, w_ref[...].astype(BF16), preferred_element_type=F32)
    finish(project(wq_ref), q_ref, Q_SCALE, qg_ref)
    finish(project(wk_ref), k_ref, 1.0, kg_ref)
    v_ref[...] = project(wv_ref).astype(BF16)


def _proj_in(mixer, x, g, w, nq, nk, cos_l, sin_l, qg, kg, seq):
    t, d = x.shape[0], x.shape[-1]
    tm = TOKEN_BLOCK
    nv = nk
    assert w.shape == (d, nq + nk + nv) and nq % nk == 0
    sblocks = seq // tm
    row = lambda i: (i, 0)
    fixed = lambda i: (0, 0)
    tab = lambda i: (i % sblocks, 0)
    return pl.pallas_call(
        functools.partial(_proj_in_body, mixer),
        grid=(t // tm,),
        in_specs=[
            _residual_spec(x, tm),
            pl.BlockSpec((1, d), fixed),
            pl.BlockSpec((d, nq), fixed),
            pl.BlockSpec((d, nk), lambda i: (0, nq // nk)),
            pl.BlockSpec((d, nv), lambda i: (0, nq // nk + 1)),
            pl.BlockSpec((tm, LANES), tab),
            pl.BlockSpec((tm, LANES), tab),
            pl.BlockSpec((1, LANES), fixed),
            pl.BlockSpec((1, LANES), fixed),
        ],
        out_specs=[
            pl.BlockSpec((tm, nq), row),
            pl.BlockSpec((tm, nk), row),
            pl.BlockSpec((tm, nv), row),
        ],
        out_shape=[
            jax.ShapeDtypeStruct((t, nq), BF16),
            jax.ShapeDtypeStruct((t, nk), BF16),
            jax.ShapeDtypeStruct((t, nv), BF16),
        ],
        scratch_shapes=[pltpu.VMEM((tm, d), F32)],
        compiler_params=_params("parallel"),
        name=f"proj_in_m{mixer}",
    )(x, g, w, w, w, cos_l, sin_l, qg, kg)


LOG2E = 1.4426950408889634
Q_SCALE = HEAD_DIM ** -0.5 * LOG2E
KEY_CHUNK = 512


def _key_chunk(span):
    assert span % KEY_CHUNK == 0
    return KEY_CHUNK


def _attention_chains(n_chains, q_of, k_of, v_aug_of, finish, s_ref, p_ref, span, mask_of=None, floor_of=None):
    kc = _key_chunk(span)
    pieces = span // kc

    def scores(c, j, q):
        s = lax.dot_general(q, k_of(c, j), NT_DIMS, preferred_element_type=F32)
        if mask_of is not None:
            s = jnp.where(mask_of(j), s, -jnp.inf)
        s_ref[c % 2, :, j * kc:(j + 1) * kc] = s

    q = q_of(0)
    for j in range(pieces):
        scores(0, j, q)
    for c in range(n_chains):
        slot = c % 2
        m = jnp.max(s_ref[slot], axis=-1, keepdims=True)
        if floor_of is not None:
            m = jnp.maximum(m, floor_of(c))
        q = q_of(c + 1) if c + 1 < n_chains else None
        for j in range(pieces):
            if q is not None:
                scores(c + 1, j, q)
            e = jnp.exp2(s_ref[slot, :, j * kc:(j + 1) * kc] - m)
            p_ref[slot, :, j * kc:(j + 1) * kc] = e.astype(BF16)
        finish(c, jnp.dot(p_ref[slot], v_aug_of(c), preferred_element_type=F32), m)


def _fill_v_aug(vaug_ref, v_ref):
    vaug_ref[:, :LANES] = v_ref[...]
    vaug_ref[:, LANES:] = jnp.ones((vaug_ref.shape[0], LANES), BF16)


def _lane_tiles(e):
    return [e[:, j * LANES:(j + 1) * LANES] for j in range(e.shape[1] // LANES)]


def _tree(op, parts):
    while len(parts) > 1:
        parts = [op(a, b) for a, b in zip(parts[::2], parts[1::2])]
    return parts[0]


def _lane_tile_sum(e):
    return _tree(jnp.add, _lane_tiles(e))


def _lane_tile_max(pieces):
    return _tree(jnp.maximum, [_tree(jnp.maximum, _lane_tiles(p)) for p in pieces])


def _diff_attn_body(lambda_init, q_ref, k_ref, v_ref, lq1_ref, lk1_ref, lq2_ref, lk2_ref, sg_ref, o_ref,
                    s00, s01, s10, s11, p0, p1, linv0, linv1, acc_ref):
    s_ref, p_ref, linv_ref = ((s00, s01), (s10, s11)), (p0, p1), (linv0, linv1)
    tq = s00.shape[0]
    span = k_ref.shape[0]
    kc = _key_chunk(span)
    pieces = span // kc
    tiles = q_ref.shape[0] // tq
    lane = lax.broadcasted_iota(I32, (tq, LANES), 1)
    lam = (jnp.exp(jnp.sum(lq1_ref[...] * lk1_ref[...], axis=-1, keepdims=True))
           - jnp.exp(jnp.sum(lq2_ref[...] * lk2_ref[...], axis=-1, keepdims=True))
           + lambda_init)

    def q_of(t, mp):
        q = q_ref[t * tq:(t + 1) * tq, :]
        keep = lane < HEAD_DIM if mp == 0 else lane >= HEAD_DIM
        return jnp.where(keep, q, jnp.zeros_like(q))

    def scores(t, mp, j, q):
        s_ref[t % 2][mp][:, j * kc:(j + 1) * kc] = lax.dot_general(
            q, k_ref[j * kc:(j + 1) * kc, :], NT_DIMS, preferred_element_type=F32)

    qs = [q_of(0, 0), q_of(0, 1)]
    for j in range(pieces):
        scores(0, 0, j, qs[0])
        scores(0, 1, j, qs[1])
    blocks = tq // SOFTMAX_ROWS
    cols = [slice(j * kc, (j + 1) * kc) for j in range(pieces)]

    def softmax_block(t, r):
        slot = t % 2
        rows = slice(r * SOFTMAX_ROWS, (r + 1) * SOFTMAX_ROWS)
        sums = []
        for mp in range(2):
            tile_max = _lane_tile_max([s_ref[slot][mp][rows, c] for c in cols])
            m = jnp.max(tile_max, axis=-1, keepdims=True)
            part = jnp.zeros((SOFTMAX_ROWS, LANES), F32)
            for c in cols:
                e = jnp.exp2(s_ref[slot][mp][rows, c] - m)
                s_ref[slot][mp][rows, c] = e
                part = part + _lane_tile_sum(e)
            sums.append(jnp.sum(part, axis=-1, keepdims=True))
        ratio = lam * sums[0] / sums[1]
        for c in cols:
            a = s_ref[slot][0][rows, c] - s_ref[slot][1][rows, c] * ratio
            p_ref[slot][rows, c] = a.astype(BF16)
        linv_ref[slot][rows, :] = jnp.broadcast_to(1.0 / sums[0], (SOFTMAX_ROWS, LANES))

    def finish(t):
        o = acc_ref[...] * linv_ref[t % 2][...]
        ms = jnp.mean(o * o, axis=-1, keepdims=True)
        o = o * lax.rsqrt(ms + RMS_EPS) * sg_ref[...] * (1.0 - lambda_init)
        o_ref[t * tq:(t + 1) * tq, :] = o.astype(BF16)

    for t in range(tiles + 1):
        ahead = []
        if t + 1 < tiles:
            qs = [q_of(t + 1, 0), q_of(t + 1, 1)]
            ahead = [(mp, j) for mp in range(2) for j in range(pieces)]
        per_block = -(-len(ahead) // blocks)
        for r in range(max(blocks, pieces)):
            for mp, j in ahead[r * per_block:(r + 1) * per_block]:
                scores(t + 1, mp, j, qs[mp])
            if t >= 1 and r < pieces:
                part_o = jnp.dot(p_ref[(t - 1) % 2][:, cols[r]], v_ref[cols[r], :], preferred_element_type=F32)
                acc_ref[...] = part_o if r == 0 else acc_ref[...] + part_o
            if t < tiles and r < blocks:
                softmax_block(t, r)
        if t >= 1:
            finish(t - 1)


DIFF_Q_TILE = 256
DIFF_Q_TILES = 4
SOFTMAX_ROWS = 32


def _attn_scratch(tq, span, keys):
    return [pltpu.VMEM((keys, 2 * LANES), BF16), pltpu.VMEM((2, tq, span), F32), pltpu.VMEM((2, tq, span), BF16)]


def _diff_attn(q, k, v, lq1, lk1, lq2, lk2, subln_g, lambda_init):
    b, s, n = q.shape
    heads = n // LANES
    tq = DIFF_Q_TILE
    rows = DIFF_Q_TILES * tq
    vec = pl.BlockSpec((1, HEAD_DIM), lambda bi, h, i: (0, 0))
    return pl.pallas_call(
        functools.partial(_diff_attn_body, lambda_init),
        grid=(b, heads, s // rows),
        in_specs=[
            pl.BlockSpec((None, rows, LANES), lambda bi, h, i: (bi, i, h)),
            pl.BlockSpec((None, s, LANES), lambda bi, h, i: (bi, 0, h)),
            pl.BlockSpec((None, s, LANES), lambda bi, h, i: (bi, 0, h)),
            vec, vec, vec, vec,
            pl.BlockSpec((1, LANES), lambda bi, h, i: (0, 0)),
        ],
        out_specs=pl.BlockSpec((None, rows, LANES), lambda bi, h, i: (bi, i, h)),
        out_shape=jax.ShapeDtypeStruct((b, s, n), BF16),
        scratch_shapes=([pltpu.VMEM((tq, s), F32)] * 4 + [pltpu.VMEM((tq, s), BF16)] * 2
                        + [pltpu.VMEM((tq, LANES), F32)] * 3),
        compiler_params=_params("parallel", "parallel", "parallel"),
        name="diff_attn",
    )(q, k, v, lq1, lk1, lq2, lk2, subln_g)


def _place_head(x, src_half, dst_half):
    return x if src_half == dst_half else pltpu.roll(x, HEAD_DIM, 1)


def _gqa_pair_body(window, q_ref, k_ref, v_ref, sink_ref, o_ref, vaug_ref, s_ref, p_ref):
    pl.when(pl.program_id(2) == 0)(functools.partial(_fill_v_aug, vaug_ref, v_ref))
    tq = q_ref.shape[0]
    span = s_ref.shape[2]
    kc = _key_chunk(span)
    group = GQA_HEADS // GQA_KV_HEADS
    pair = pl.program_id(1)
    lane = lax.broadcasted_iota(I32, (tq, LANES), 1)
    if window:
        i = pl.program_id(2)
        start = pl.multiple_of(jnp.clip(i * tq - WINDOW, 0, k_ref.shape[0] - span), LANES)
        qpos = i * tq + lax.broadcasted_iota(I32, (tq, kc), 0)
        kpos = start + lax.broadcasted_iota(I32, (tq, kc), 1)
        mask_of = lambda j: jnp.abs(qpos - (kpos + j * kc)) <= WINDOW
        k_of = lambda c, j: k_ref[pl.ds(start + j * kc, kc), :]
        v_aug_of = lambda c: vaug_ref[pl.ds(start, span), :]
        sink2 = lambda c: sink_ref[pair * 2 * group + c] * LOG2E
    else:
        mask_of = sink2 = None
        k_of = lambda c, j: k_ref[j * kc:(j + 1) * kc, :]
        v_aug_of = lambda c: vaug_ref[...]

    def q_of(c):
        hh, kvl = c % 2, c // group
        qc = q_ref[:, (c // 2) * LANES:(c // 2 + 1) * LANES]
        qh = jnp.where((lane >= hh * HEAD_DIM) & (lane < (hh + 1) * HEAD_DIM), qc, jnp.zeros_like(qc))
        return qh if hh == kvl else _place_head(qh.astype(F32), hh, kvl).astype(BF16)

    first = []

    def finish(c, o_aug, m):
        hh, kvl = c % 2, c // group
        total = o_aug[:, LANES:LANES + 1]
        if window:
            total = total + jnp.exp2(sink2(c) - m)
        o = _place_head(o_aug[:, :LANES] * (1.0 / total), kvl, hh)
        if hh == 0:
            first.append(o)
        else:
            o_ref[:, (c // 2) * LANES:(c // 2 + 1) * LANES] = jnp.where(lane < HEAD_DIM, first.pop(), o).astype(BF16)

    _attention_chains(2 * group, q_of, k_of, v_aug_of, finish, s_ref, p_ref, span, mask_of, sink2)


def _gqa_attn(q, k, v, sinks, window):
    b, s, n = q.shape
    pairs = GQA_KV_HEADS // 2
    qw = n // pairs
    tq = Q_TILE
    span = tq + 2 * WINDOW if window else s
    return pl.pallas_call(
        functools.partial(_gqa_pair_body, window),
        grid=(b, pairs, s // tq),
        in_specs=[
            pl.BlockSpec((None, tq, qw), lambda bi, p, i: (bi, i, p)),
            pl.BlockSpec((None, s, LANES), lambda bi, p, i: (bi, 0, p)),
            pl.BlockSpec((None, s, LANES), lambda bi, p, i: (bi, 0, p)),
            pl.BlockSpec(memory_space=pltpu.SMEM),
        ],
        out_specs=pl.BlockSpec((None, tq, qw), lambda bi, p, i: (bi, i, p)),
        out_shape=jax.ShapeDtypeStruct((b, s, n), BF16),
        scratch_shapes=_attn_scratch(tq, span, s),
        compiler_params=_params("parallel", "parallel", "arbitrary"),
        name="window_attn" if window else "axial_attn",
    )(q, k, v, sinks)


def _store_rows(ref, val):
    ref[...] = val.reshape(ref.shape)


def _proj_out_body(o_ref, wo_ref, x_ref, g_ref, wr_ref, xn_ref, hp_ref, aff_ref, x2_ref):
    xn = _load_residual(x_ref, x2_ref) + jnp.dot(o_ref[...], wo_ref[...].astype(BF16),
                                                  preferred_element_type=F32)
    _store_rows(xn_ref, xn)
    ms = jnp.mean(xn * xn, axis=-1, keepdims=True)
    h = xn * lax.rsqrt(ms + RMS_EPS) * g_ref[...]
    hb = h.astype(BF16)
    hb32 = hb.astype(F32)
    half = h.shape[1] // 2
    hi = lax.bitcast_convert_type(hb32[:, :half], jnp.uint32)
    lo = lax.bitcast_convert_type(hb32[:, half:], jnp.uint32)
    _store_rows(hp_ref, hi | (lo >> 16))
    wr = wr_ref[...]
    wr_hi = wr.astype(BF16)
    wr_lo = (wr - wr_hi.astype(F32)).astype(BF16)
    h_lo = (h - hb32).astype(BF16)
    logits = (lax.dot_general(wr_hi, hb, NT_DIMS, preferred_element_type=F32)
              + lax.dot_general(wr_lo, hb, NT_DIMS, preferred_element_type=F32)
              + lax.dot_general(wr_hi, h_lo, NT_DIMS, preferred_element_type=F32))
    m = jnp.max(logits, axis=0, keepdims=True)
    e = jnp.exp(logits - m)
    aff_ref[...] = e / jnp.sum(e, axis=0, keepdims=True)


def _proj_out(o, wo, x, g, wr_t, seq):
    t, d = x.shape[0], x.shape[-1]
    tm = TOKEN_BLOCK
    e = wr_t.shape[0]
    sblocks = seq // tm
    row = lambda i: (i, 0)
    fixed = lambda i: (0, 0)
    return pl.pallas_call(
        _proj_out_body,
        grid=(t // tm,),
        in_specs=[
            pl.BlockSpec((tm, o.shape[1]), row),
            pl.BlockSpec(wo.shape, fixed),
            _residual_spec(x, tm),
            pl.BlockSpec((1, d), fixed),
            pl.BlockSpec((e, d), fixed),
        ],
        out_specs=[
            pl.BlockSpec((tm, 1, d), lambda i: (i, 0, 0)),
            pl.BlockSpec((tm, 1, d // 2), lambda i: (i, 0, 0)),
            pl.BlockSpec((None, e, tm), lambda i: (i // sblocks, 0, i % sblocks)),
        ],
        out_shape=[
            jax.ShapeDtypeStruct((t, 1, d), F32),
            jax.ShapeDtypeStruct((t, 1, d // 2), jnp.uint32),
            jax.ShapeDtypeStruct((t // seq, e, seq), F32),
        ],
        scratch_shapes=[pltpu.VMEM((tm, d), F32)],
        compiler_params=_params("parallel"),
        name="proj_out_router",
    )(o, wo, x, g, wr_t)


PREFIX_CHUNK = 256


def _exclusive_prefix(mask_f):
    e, s = mask_f.shape
    r = lax.broadcasted_iota(I32, (PREFIX_CHUNK, PREFIX_CHUNK), 0)
    c = lax.broadcasted_iota(I32, (PREFIX_CHUNK, PREFIX_CHUNK), 1)
    upper = jnp.where(r < c, 1.0, 0.0).astype(BF16)
    offset = jnp.zeros((e, 1), F32)
    outs = []
    for j in range(s // PREFIX_CHUNK):
        mj = mask_f[:, j * PREFIX_CHUNK:(j + 1) * PREFIX_CHUNK]
        outs.append(jnp.dot(mj.astype(BF16), upper, preferred_element_type=F32) + offset)
        offset = offset + jnp.sum(mj, axis=-1, keepdims=True)
    return jnp.concatenate(outs, axis=1)


def _route_body(cap, aff_ref, out_ref, slot_ref):
    aff = aff_ref[...]
    e, s = aff.shape
    bits = lax.bitcast_convert_type(aff, I32)

    def count(mask):
        return jnp.sum(jnp.where(mask, 1.0, 0.0), axis=-1, keepdims=True)

    def search(i, thr):
        cand = thr | lax.shift_left(jnp.int32(1), 30 - i)
        return jnp.where(count(bits >= cand) >= cap, cand, thr)

    thr = lax.fori_loop(0, 31, search, jnp.zeros((e, 1), I32))
    gt = bits > thr
    eq = bits == thr
    need = cap - count(gt)
    eq_rank = _exclusive_prefix(jnp.where(eq, 1.0, 0.0))
    sel = gt | (eq & (eq_rank < need))
    sel_f = jnp.where(sel, 1.0, 0.0)
    slot_ref[...] = jnp.where(sel, _exclusive_prefix(sel_f), -1.0)

    slot_id = lax.broadcasted_iota(I32, (cap, s), 0).astype(F32)
    token = lax.broadcasted_iota(I32, (1, s), 1).astype(F32)
    col = lax.broadcasted_iota(I32, (cap, out_ref.shape[2]), 1)
    expert = lax.broadcasted_iota(I32, (e, s), 0)

    def per_expert(ei, carry):
        mine = expert == ei
        slot_row = jnp.sum(jnp.where(mine, slot_ref[...], 0.0), axis=0, keepdims=True)
        aff_row = jnp.sum(jnp.where(mine, aff_ref[...], 0.0), axis=0, keepdims=True)
        hit = slot_row == slot_id
        idx = jnp.sum(jnp.where(hit, token, 0.0), axis=-1, keepdims=True)
        gate = jnp.sum(jnp.where(hit, aff_row, 0.0), axis=-1, keepdims=True)
        out_ref[ei] = jnp.where(col == 0, idx, gate)
        return carry

    lax.fori_loop(0, e, per_expert, 0)


ROUTE_COLS = 8


def _route(aff, cap):
    b, e, s = aff.shape
    return pl.pallas_call(
        functools.partial(_route_body, cap),
        grid=(b,),
        in_specs=[pl.BlockSpec((None, e, s), lambda bi: (bi, 0, 0))],
        out_specs=pl.BlockSpec((None, e, cap, ROUTE_COLS), lambda bi: (bi, 0, 0, 0)),
        out_shape=jax.ShapeDtypeStruct((b, e, cap, ROUTE_COLS), F32),
        scratch_shapes=[pltpu.VMEM((e, s), F32)],
        compiler_params=_params("parallel"),
        name="ec_route",
    )(aff)


ROW_UNROLL = 8
COMBINE_EXPERTS = 4


def _gather_body(idx_ref, h_ref, o_ref):
    n = o_ref.shape[0]

    def rows(j, carry):
        base = j * ROW_UNROLL
        for u in range(ROW_UNROLL):
            o_ref[base + u] = h_ref[idx_ref[0, base + u]]
        return carry

    lax.fori_loop(0, n // ROW_UNROLL, rows, 0)


def _gather_rows(idx, hp):
    b, s, _, w = hp.shape
    n = idx.shape[2]
    return pl.pallas_call(
        _gather_body,
        grid=(b,),
        in_specs=[
            pl.BlockSpec((None, 1, n), lambda bi: (bi, 0, 0), memory_space=pltpu.SMEM),
            pl.BlockSpec((None, s, 1, w), lambda bi: (bi, 0, 0, 0)),
        ],
        out_specs=pl.BlockSpec((None, n, 1, w), lambda bi: (bi, 0, 0, 0)),
        out_shape=jax.ShapeDtypeStruct((b, n, 1, w), hp.dtype),
        compiler_params=_params("parallel"),
        name="ec_gather",
    )(idx, hp)


def _expert_body(xp_ref, gate_ref, wg_ref, wu_ref, wd_ref, y_ref, xp2_ref):
    rows, cap = xp_ref.shape[0], xp_ref.shape[1]
    for r in range(rows):
        xp2_ref[r * cap:(r + 1) * cap, :] = xp_ref[r].reshape(cap, xp2_ref.shape[1])
    xp = xp2_ref[...]
    x_hi = lax.bitcast_convert_type(xp & jnp.uint32(0xFFFF0000), F32).astype(BF16)
    x_lo = lax.bitcast_convert_type(xp << 16, F32).astype(BF16)
    xe = jnp.concatenate([x_hi, x_lo], axis=1)
    hg = jnp.dot(xe, wg_ref[...], preferred_element_type=F32)
    hu = jnp.dot(xe, wu_ref[...], preferred_element_type=F32)
    hid = (hg * jax.nn.sigmoid(hg) * hu).astype(BF16)
    y = jnp.dot(hid, wd_ref[...], preferred_element_type=F32)
    for r in range(rows):
        _store_rows(y_ref.at[r], y[r * cap:(r + 1) * cap, :] * gate_ref[r][:, 1:2])


EXPERT_BATCH_ROWS = 2


def _experts(xg, route, wg, wu, wd, layer):
    b, n, _, half = xg.shape
    _, e, d, f = wg.shape
    cap = n // e
    rows = EXPERT_BATCH_ROWS
    return pl.pallas_call(
        _expert_body,
        grid=(e, b // rows),
        in_specs=[
            pl.BlockSpec((rows, cap, 1, half), lambda ei, bi: (bi, ei, 0, 0)),
            pl.BlockSpec((rows, None, cap, ROUTE_COLS), lambda ei, bi: (bi, ei, 0, 0)),
            pl.BlockSpec((None, None, d, f), lambda ei, bi: (layer, ei, 0, 0)),
            pl.BlockSpec((None, None, d, f), lambda ei, bi: (layer, ei, 0, 0)),
            pl.BlockSpec((None, None, f, d), lambda ei, bi: (layer, ei, 0, 0)),
        ],
        out_specs=pl.BlockSpec((rows, cap, 1, d), lambda ei, bi: (bi, ei, 0, 0)),
        out_shape=jax.ShapeDtypeStruct((b, n, 1, d), F32),
        scratch_shapes=[pltpu.VMEM((rows * cap, half), jnp.uint32)],
        compiler_params=_params("parallel", "parallel"),
        name="ec_experts",
    )(xg, route, wg, wu, wd)


def _combine_body(idx_ref, x_ref, y_ref, o_ref):
    @pl.when(pl.program_id(1) == 0)
    def _():
        o_ref[...] = x_ref[...]

    def rows(j, carry):
        base = j * ROW_UNROLL
        sums = [o_ref[idx_ref[0, base + u]] + y_ref[base + u] for u in range(ROW_UNROLL)]
        for u in range(ROW_UNROLL):
            o_ref[idx_ref[0, base + u]] = sums[u]
        return carry

    lax.fori_loop(0, y_ref.shape[0] // ROW_UNROLL, rows, 0)


def _combine(idx, x, y, cap):
    b, s, _, d = x.shape
    rows = COMBINE_EXPERTS * cap
    assert cap % ROW_UNROLL == 0 and y.shape[1] % rows == 0
    return pl.pallas_call(
        _combine_body,
        grid=(b, y.shape[1] // rows),
        in_specs=[
            pl.BlockSpec((None, 1, rows), lambda bi, ei: (bi, 0, ei), memory_space=pltpu.SMEM),
            pl.BlockSpec((None, s, 1, d), lambda bi, ei: (bi, 0, 0, 0)),
            pl.BlockSpec((None, rows, 1, d), lambda bi, ei: (bi, ei, 0, 0)),
        ],
        out_specs=pl.BlockSpec((None, s, 1, d), lambda bi, ei: (bi, 0, 0, 0)),
        out_shape=jax.ShapeDtypeStruct(x.shape, F32),
        compiler_params=_params("parallel", "arbitrary"),
        name="ec_combine",
    )(idx, x, y)


def _final_norm_body(x_ref, g_ref, o_ref):
    o_ref[...] = x_ref[...].reshape(o_ref.shape)
    x = o_ref[...]
    ms = jnp.mean(x * x, axis=-1, keepdims=True)
    o_ref[...] = x * lax.rsqrt(ms + RMS_EPS) * g_ref[...]


def _final_norm(x, g):
    t, _, d = x.shape
    tm = TOKEN_BLOCK
    return pl.pallas_call(
        _final_norm_body,
        grid=(t // tm,),
        in_specs=[pl.BlockSpec((tm, 1, d), lambda i: (i, 0, 0)), pl.BlockSpec((1, d), lambda i: (0, 0))],
        out_specs=pl.BlockSpec((tm, d), lambda i: (i, 0)),
        out_shape=jax.ShapeDtypeStruct((t, d), F32),
        compiler_params=_params("parallel"),
        name="final_norm",
    )(x, g)


def kernel(x, attn_norm_g, ffn_norm_g, final_norm_g, a_w_in, a_w_o, a_lambda_q1, a_lambda_k1, a_lambda_q2,
           a_lambda_k2, a_subln_g, b_w_in, b_w_o, b_q_norm_g, b_k_norm_g, c_w_in, c_w_o, c_sinks,
           moe_w_router, moe_w_gate, moe_w_up, moe_w_down):
    batch, seq, d = x.shape
    depth = attn_norm_g.shape[0]
    t = batch * seq
    cap = EC_CAPACITY_FACTOR * seq // N_EXPERTS
    gqa_q = GQA_HEADS * HEAD_DIM
    gqa_kv = GQA_KV_HEADS * HEAD_DIM

    tables_1d = _lane_tables_1d(seq)
    tables_ax = _lane_tables_axial(seq)
    ones = jnp.ones((1, LANES), F32)
    w_gate, w_up, w_down = moe_w_gate.astype(BF16), moe_w_up.astype(BF16), moe_w_down.astype(BF16)

    xt = x.reshape(t, d)
    for i in range(depth):
        m, j = i % N_MIXERS, i // N_MIXERS
        g_attn = attn_norm_g[i].reshape(1, d)
        if m == 0:
            q, k, v = _proj_in(0, xt, g_attn, a_w_in[j], d, d, *tables_1d, ones, ones, seq)
            lambda_init = 0.8 - 0.6 * math.exp(-0.3 * i)
            vec = lambda a: a[j].reshape(1, HEAD_DIM)
            o = _diff_attn(q.reshape(batch, seq, d), k.reshape(batch, seq, d), v.reshape(batch, seq, d),
                           vec(a_lambda_q1), vec(a_lambda_k1), vec(a_lambda_q2), vec(a_lambda_k2),
                           a_subln_g[j].reshape(1, 2 * HEAD_DIM), lambda_init)
            w_o = a_w_o[j]
        else:
            w_in = b_w_in[j] if m == 1 else c_w_in[j]
            if m == 1:
                tables = tables_ax
                qg = jnp.tile(b_q_norm_g[j], 2).reshape(1, LANES)
                kg = jnp.tile(b_k_norm_g[j], 2).reshape(1, LANES)
                sinks = jnp.zeros((GQA_HEADS,), F32)
            else:
                tables, qg, kg, sinks = tables_1d, ones, ones, c_sinks[j]
            q, k, v = _proj_in(m, xt, g_attn, w_in, gqa_q, gqa_kv, *tables, qg, kg, seq)
            o = _gqa_attn(q.reshape(batch, seq, gqa_q), k.reshape(batch, seq, gqa_kv),
                          v.reshape(batch, seq, gqa_kv), sinks, window=(m == 2))
            w_o = b_w_o[j] if m == 1 else c_w_o[j]

        xn, hp, aff = _proj_out(o.reshape(t, -1), w_o, xt, ffn_norm_g[i].reshape(1, d),
                                moe_w_router[i].T, seq)
        route = _route(aff, cap)
        idx = route[..., 0].astype(I32).reshape(batch, 1, N_EXPERTS * cap)
        xg = _gather_rows(idx, hp.reshape(batch, seq, 1, d // 2))
        y = _experts(xg, route, w_gate, w_up, w_down, i)
        xo = _combine(idx, xn.reshape(batch, seq, 1, d), y, cap)
        xt = xo.reshape(t, 1, d)
    return _final_norm(xt, final_norm_g.reshape(1, d)).reshape(batch, seq, d)
```
